```python
import math
import jax, jax.numpy as jnp
from jax import lax
import numpy as np

D_MODEL = 1024
BATCH = 2
SEQ = 8192
DEPTH = 4
DEC_BATCH = 128
DEC_SEQ = 4
PAST_LEN = 8192
PAGE_SIZE = 128

N_A = DEPTH // 2
N_B = DEPTH - N_A
POOL_WINDOWS = (2, 4, 8, 16)
N_POOL_GROUPS = len(POOL_WINDOWS)
POOL_GROUP = D_MODEL // N_POOL_GROUPS
POOL_BUF = max(POOL_WINDOWS) - 1
HEAD_DIM = 64
N_HEADS = D_MODEL // HEAD_DIM
N_KV = 4
GROUP = N_HEADS // N_KV
WINDOW = 128
BLOCK = WINDOW
N_BUCKETS = 32
MAX_DISTANCE = 128
D_FF = 2816
CONV_W = 3
CONV_BUF = CONV_W - 1
PLE_DIM = 256
EPS = 1e-6

kernel_name = 'yoco_pool_swa_sink_convffn_step'


def rmsnorm(x, g):
    xf = x.astype(jnp.float32)
    y = xf * lax.rsqrt(jnp.mean(xf * xf, axis=-1, keepdims=True) + EPS)
    return (y * g.astype(jnp.float32)).astype(x.dtype)


def pool_mixer(xn, prefix, start, w_grp, scale):
    N, L, _ = xn.shape
    ext = jnp.concatenate([prefix.astype(xn.dtype), xn], axis=1).astype(jnp.float32)
    c = jnp.pad(jnp.cumsum(ext, axis=1), ((0, 0), (1, 0), (0, 0)))
    pos = start + jnp.arange(L)
    outs = []
    for g, w in enumerate(POOL_WINDOWS):
        sl = slice(g * POOL_GROUP, (g + 1) * POOL_GROUP)
        cg = c[:, :, sl]
        s = cg[:, POOL_BUF + 1:] - cg[:, POOL_BUF + 1 - w:POOL_BUF + 1 - w + L]
        cnt = jnp.minimum(pos + 1, w).astype(jnp.float32)
        outs.append(s / cnt[None, :, None] - ext[:, POOL_BUF:, sl])
    d = jnp.stack(outs, axis=2).astype(xn.dtype)
    y = jnp.einsum('nlgc,gcd->nlgd', d, w_grp).reshape(N, L, D_MODEL)
    return y * scale


def conv_ffn(xn, prefix, w_up, conv_w, conv_b, w_down):
    L = xn.shape[1]
    u = xn @ w_up
    ext = jnp.concatenate([prefix.astype(u.dtype), u], axis=1)
    c = conv_b + ext[:, 0:L] * conv_w[0]
    for k in range(1, CONV_W):
        c = c + ext[:, k:k + L] * conv_w[k]
    gate, val = jnp.split(c, 2, axis=-1)
    return (jax.nn.gelu(gate) * val) @ w_down, ext[:, -CONV_BUF:]


def t5_bucket(d):
    n = jnp.maximum(d, 0)
    max_exact = N_BUCKETS // 2
    nf = jnp.maximum(n, 1).astype(jnp.float32)
    large = max_exact + (jnp.log(nf / max_exact) / math.log(MAX_DISTANCE / max_exact)
                         * (N_BUCKETS - max_exact)).astype(jnp.int32)
    large = jnp.minimum(large, N_BUCKETS - 1)
    return jnp.where(n < max_exact, n, large)


def window_attention(q, k, v, bias, valid, sink):
    s = jnp.einsum('nbqhgd,nbjhd->nbhgqj', q, k).astype(jnp.float32) * (HEAD_DIM ** -0.5)
    s = jnp.where(valid[None, :, None, None], s + bias, -jnp.inf)
    sk = sink.astype(jnp.float32)[None, None, :, :, None, None]
    m = jnp.maximum(jnp.max(s, axis=-1, keepdims=True), sk)
    e = jnp.exp(s - m)
    p = e / (jnp.sum(e, axis=-1, keepdims=True) + jnp.exp(sk - m))
    return jnp.einsum('nbhgqj,nbjhd->nbqhgd', p.astype(v.dtype), v)


def shared_kv(h, kv_norm, w_k, w_v, k_norm):
    N, L, _ = h.shape
    xn = rmsnorm(h, kv_norm)
    k = rmsnorm((xn @ w_k).reshape(N, L, N_KV, HEAD_DIM), k_norm)
    v = (xn @ w_v).reshape(N, L, N_KV, HEAD_DIM)
    return k, v


def trunk(x, p, pool_prev, conv_prev, k_prev, v_prev, start, prompt, W):
    N, L, _ = x.shape
    h = x
    pool_new, conv_new = [], []
    nblk, qlen = (L // BLOCK, BLOCK) if prompt else (1, L)
    for i in range(DEPTH):
        if i == N_A:
            k_new, v_new = shared_kv(h, W['kv_norm'], W['w_k'], W['w_v'], W['k_norm'])
            if prompt:
                kb = k_new.reshape(N, nblk, BLOCK, N_KV, HEAD_DIM)
                vb = v_new.reshape(N, nblk, BLOCK, N_KV, HEAD_DIM)
                padb = ((0, 0), (1, 0), (0, 0), (0, 0), (0, 0))
                k_blk = jnp.concatenate([jnp.pad(kb[:, :-1], padb), kb], axis=2)
                v_blk = jnp.concatenate([jnp.pad(vb[:, :-1], padb), vb], axis=2)
                k_win, v_win = k_new[:, -WINDOW:], v_new[:, -WINDOW:]
                kj = jnp.arange(2 * BLOCK)[None, :]
                d = jnp.arange(BLOCK)[:, None] + BLOCK - kj
                valid = ((d >= 0) & (d < WINDOW))[None] & (
                    (jnp.arange(nblk)[:, None, None] > 0) | (kj >= BLOCK)[None])
            else:
                k_ext = jnp.concatenate([k_prev.astype(k_new.dtype), k_new], axis=1)
                v_ext = jnp.concatenate([v_prev.astype(v_new.dtype), v_new], axis=1)
                k_win, v_win = k_ext[:, -WINDOW:], v_ext[:, -WINDOW:]
                k_blk, v_blk = k_ext[:, None], v_ext[:, None]
                d = jnp.arange(L)[:, None] + WINDOW - jnp.arange(WINDOW + L)[None, :]
                valid = ((d >= 0) & (d < WINDOW))[None]
            bias = jnp.transpose(W['rel_bias'][t5_bucket(d)], (2, 0, 1))
            bias = bias.reshape(N_KV, GROUP, qlen, -1).astype(jnp.float32)
        xn = rmsnorm(h, W['norm_mix'][i])
        if i < N_A:
            pre = jnp.zeros((N, POOL_BUF, D_MODEL), xn.dtype) if prompt else pool_prev[i].astype(xn.dtype)
            h = h + pool_mixer(xn, pre, start, W['w_pool'][i], W['pool_scale'][i])
            pool_new.append(jnp.concatenate([pre, xn], axis=1)[:, -POOL_BUF:])
        else:
            j = i - N_A
            q = (xn @ W['w_q'][j]).reshape(N, nblk, qlen, N_KV, GROUP, HEAD_DIM)
            q = rmsnorm(q, W['q_norm'][j])
            o = window_attention(q, k_blk, v_blk, bias, valid, W['sinks'][j].reshape(N_KV, GROUP))
            h = h + o.reshape(N, L, D_MODEL) @ W['w_o'][j]
        xn = rmsnorm(h, W['norm_ffn'][i])
        cpre = jnp.zeros((N, CONV_BUF, 2 * D_FF), xn.dtype) if prompt else conv_prev[i]
        f, cstate = conv_ffn(xn, cpre, W['w_up'][i], W['conv_w'][i], W['conv_b'][i], W['w_down'][i])
        h = h + f
        conv_new.append(cstate)
        gate = jax.nn.sigmoid(rmsnorm(h, W['norm_ple'][i]) @ W['w_ple_gate'][i])
        h = h + gate * (p[i] @ W['w_ple_proj'][i])
    return h, jnp.stack(pool_new), jnp.stack(conv_new), k_win, v_win


def setup_inputs(seed: int = 0) -> dict:
    key = jax.random.key(seed)
    ks = iter(jax.random.split(key, 40))

    def nrm(shape, scale):
        return jax.random.normal(next(ks), shape, jnp.float32) * scale

    F2 = 2 * D_FF
    return {
        'x_prompt': nrm((BATCH, SEQ, D_MODEL), 1.0),
        'x_sample': nrm((DEC_BATCH, DEC_SEQ, D_MODEL), 1.0),
        'p_prompt': nrm((DEPTH, BATCH, SEQ, PLE_DIM), 1.0),
        'p_sample': nrm((DEPTH, DEC_BATCH, DEC_SEQ, PLE_DIM), 1.0),
        'state_pool': nrm((N_A, DEC_BATCH, POOL_BUF, D_MODEL), 1.0),
        'state_conv': nrm((DEPTH, DEC_BATCH, CONV_BUF, F2), 1.0),
        'cache_k': nrm((DEC_BATCH, WINDOW, N_KV, HEAD_DIM), 1.0),
        'cache_v': nrm((DEC_BATCH, WINDOW, N_KV, HEAD_DIM), 1.0),
        'norm_mix': 1.0 + nrm((DEPTH, D_MODEL), 0.05),
        'norm_ffn': 1.0 + nrm((DEPTH, D_MODEL), 0.05),
        'norm_ple': 1.0 + nrm((DEPTH, D_MODEL), 0.05),
        'w_pool': nrm((N_A, N_POOL_GROUPS, POOL_GROUP, POOL_GROUP), POOL_GROUP ** -0.5),
        'pool_scale': 1.0 + nrm((N_A, D_MODEL), 0.1),
        'kv_norm': 1.0 + nrm((D_MODEL,), 0.05),
        'w_k': nrm((D_MODEL, N_KV * HEAD_DIM), D_MODEL ** -0.5),
        'w_v': nrm((D_MODEL, N_KV * HEAD_DIM), D_MODEL ** -0.5),
        'k_norm': 1.0 + nrm((HEAD_DIM,), 0.05),
        'w_q': nrm((N_B, D_MODEL, D_MODEL), D_MODEL ** -0.5),
        'q_norm': 1.0 + nrm((N_B, HEAD_DIM), 0.05),
        'sinks': nrm((N_B, N_HEADS), 0.5),
        'w_o': nrm((N_B, D_MODEL, D_MODEL), D_MODEL ** -0.5),
        'rel_bias': nrm((N_BUCKETS, N_HEADS), 0.5),
        'w_up': nrm((DEPTH, D_MODEL, F2), D_MODEL ** -0.5),
        'conv_w': nrm((DEPTH, CONV_W, F2), 0.5),
        'conv_b': nrm((DEPTH, F2), 0.02),
        'w_down': nrm((DEPTH, D_FF, D_MODEL), D_FF ** -0.5),
        'w_ple_gate': nrm((DEPTH, D_MODEL, D_MODEL), D_MODEL ** -0.5),
        'w_ple_proj': nrm((DEPTH, PLE_DIM, D_MODEL), PLE_DIM ** -0.5),
    }


def reference(x_prompt, x_sample, p_prompt, p_sample, state_pool, state_conv, cache_k, cache_v,
              norm_mix, norm_ffn, norm_ple, w_pool, pool_scale, kv_norm, w_k, w_v, k_norm,
              w_q, q_norm, sinks, w_o, rel_bias, w_up, conv_w, conv_b, w_down, w_ple_gate, w_ple_proj):
    W = dict(norm_mix=norm_mix, norm_ffn=norm_ffn, norm_ple=norm_ple, w_pool=w_pool,
             pool_scale=pool_scale, kv_norm=kv_norm, w_k=w_k, w_v=w_v, k_norm=k_norm,
             w_q=w_q, q_norm=q_norm, sinks=sinks, w_o=w_o, rel_bias=rel_bias, w_up=w_up,
             conv_w=conv_w, conv_b=conv_b, w_down=w_down, w_ple_gate=w_ple_gate,
             w_ple_proj=w_ple_proj)
    y_prompt, pool_p, conv_p, k_p, v_p = trunk(x_prompt, p_prompt, None, None, None, None,
                                               0, True, W)
    y_sample, pool_s, conv_s, k_s, v_s = trunk(x_sample, p_sample, state_pool, state_conv,
                                               cache_k, cache_v, PAST_LEN, False, W)
    return (y_prompt, y_sample, pool_p, pool_s, conv_p, conv_s, k_p, k_s, v_p, v_s)
```

```python
import functools
import math

import numpy as np
import jax
import jax.numpy as jnp
from jax import lax
from jax.experimental import pallas as pl
from jax.experimental.pallas import tpu as pltpu

D_MODEL = 1024
DEPTH = 4
N_A = DEPTH // 2
POOL_WINDOWS = (2, 4, 8, 16)
POOL_GROUP = D_MODEL // len(POOL_WINDOWS)
POOL_BUF = max(POOL_WINDOWS) - 1
HEAD_DIM = 64
N_HEADS = D_MODEL // HEAD_DIM
N_KV = 4
GROUP = N_HEADS // N_KV
KV_DIM = N_KV * HEAD_DIM
WINDOW = 128
BLOCK = WINDOW
N_BUCKETS = 32
MAX_DISTANCE = 128
D_FF = 2816
F2 = 2 * D_FF
CONV_W = 3
CONV_BUF = CONV_W - 1
PLE_DIM = 256
EPS = 1e-6
PAST_LEN = 8192

LANES = 128
SUBLANES = 8
VMEM_LIMIT_BYTES = 58 * 1024 * 1024

PROMPT_TILE = 256
UP_CHUNK = 512
FF_CHUNK = 256
DECODE_SEQ_TILE = 32
SEQ_CHUNK = 16
PAIR_ROWS = 2 * 4

F32 = jnp.float32
BF16 = jnp.bfloat16
NEG_INF = float("-inf")


def _rmsnorm(x, g):
    ms = jnp.mean(x * x, axis=-1, keepdims=True)
    return x * lax.rsqrt(ms + EPS) * g


def _gelu_tanh(x):
    c = np.float32(math.sqrt(2.0 / math.pi))
    cdf = 0.5 * (1.0 + jnp.tanh(c * (x + 0.044715 * (x * x * x))))
    return x * cdf


def _sigmoid(x):
    return 1.0 / (1.0 + jnp.exp(-x))


def _dot(a, b):
    return jnp.dot(a, b, preferred_element_type=F32)


def _dot_nt(a, b):
    return lax.dot_general(a, b, (((1,), (1,)), ((), ())), preferred_element_type=F32)


def _softmax_sink(s, sink):
    m = jnp.maximum(jnp.max(s, axis=-1, keepdims=True), sink)
    e = jnp.exp(s - m)
    denom = jnp.sum(e, axis=-1, keepdims=True) + jnp.exp(sink - m)
    return e * (1.0 / denom)


def _pool_mixer(h, norm_g, ext_ref, wpool_ref, scale, inv_cnt, pre, step, rows):
    xn = _rmsnorm(h, norm_g)
    ext_ref[pre:pre + rows, :] = xn
    outs = []
    for g, w in enumerate(POOL_WINDOWS):
        cols = slice(g * POOL_GROUP, (g + 1) * POOL_GROUP)
        cur = xn[:, cols]
        s = cur
        for k in range(1, w):
            off = pre - k * step
            s = s + ext_ref[off:off + rows, cols]
        d = s * inv_cnt[g] - cur
        outs.append(_dot(d.astype(BF16), wpool_ref[g]))
    return h + jnp.concatenate(outs, axis=1) * scale


def _conv_ffn(h, norm_g, u_ref, a_ref, wup_ref, cw_ref, cb_ref, wdown_ref, cstate_ref,
              pre, step, rows, carry):
    xn = _rmsnorm(h, norm_g).astype(BF16)
    for c in range(F2 // UP_CHUNK):
        cols = slice(c * UP_CHUNK, (c + 1) * UP_CHUNK)
        u_ref[pre:pre + rows, cols] = _dot(xn, wup_ref[:, cols])
    cstate_ref[...] = u_ref[pre + rows - CONV_BUF * step:pre + rows, :].reshape(cstate_ref.shape)

    def conv(cols):
        acc = cb_ref[:, cols] + u_ref[pre - 2 * step:pre - 2 * step + rows, cols] * cw_ref[0:1, cols]
        acc = acc + u_ref[pre - step:pre - step + rows, cols] * cw_ref[1:2, cols]
        return acc + u_ref[pre:pre + rows, cols] * cw_ref[2:3, cols]

    for c in range(D_FF // FF_CHUNK):
        gate = conv(slice(c * FF_CHUNK, (c + 1) * FF_CHUNK))
        val = conv(slice(D_FF + c * FF_CHUNK, D_FF + (c + 1) * FF_CHUNK))
        a_ref[:, c * FF_CHUNK:(c + 1) * FF_CHUNK] = (_gelu_tanh(gate) * val).astype(BF16)
    if carry:
        u_ref[0:pre, :] = u_ref[rows:rows + pre, :]
    return h + _dot(a_ref[...], wdown_ref[...])


def _ple(h, norm_g, p, wg_ref, wp_ref):
    xn = _rmsnorm(h, norm_g).astype(BF16)
    gate = _sigmoid(_dot(xn, wg_ref[...]))
    return h + gate * _dot(p.astype(BF16), wp_ref[...])


def _q_heads(h, norm_g, wq_ref, bd_ref, qnw):
    xn = _rmsnorm(h, norm_g).astype(BF16)
    q = _dot(xn, wq_ref[...])
    ssq = _dot((q * q).astype(BF16), bd_ref[...])
    return q * lax.rsqrt(ssq * (1.0 / HEAD_DIM) + EPS) * (qnw * HEAD_DIM ** -0.5)


def _ffn_ple_tail(h, t, refs, pre_c, step, rows, prompt):
    (p_ref, nffn_ref, nple_ref, wup_ref, cw_ref, cb_ref, wdown_ref, wg_ref, wp_ref,
     cpre_ref, hout_ref, cstate_ref, u_ref, a_ref) = refs
    if prompt:
        @pl.when(t == 0)
        def _():
            u_ref[0:pre_c, :] = jnp.zeros((pre_c, F2), F32)
    else:
        u_ref[0:pre_c, :] = cpre_ref[...].reshape(pre_c, F2)
    h = _conv_ffn(h, nffn_ref[...], u_ref, a_ref, wup_ref, cw_ref, cb_ref, wdown_ref, cstate_ref,
                  pre_c, step, rows, carry=prompt)
    p = p_ref[...].reshape(rows, PLE_DIM)
    hout_ref[...] = _ple(h, nple_ref[...], p, wg_ref, wp_ref).reshape(hout_ref.shape)


def _pool_layer_kernel(*refs, rows, step, pre_p, pre_c, prompt):
    it = iter(refs)
    h_ref, p_ref = next(it), next(it)
    ppre_ref = None if prompt else next(it)
    cpre_ref = None if prompt else next(it)
    nmix_ref, wpool_ref, pscale_ref = next(it), next(it), next(it)
    nffn_ref, nple_ref, wup_ref, cw_ref, cb_ref, wdown_ref, wg_ref, wp_ref = (next(it) for _ in range(8))
    hout_ref, pstate_ref, cstate_ref = next(it), next(it), next(it)
    ext_ref, u_ref, a_ref = next(it), next(it), next(it)

    t = pl.program_id(1)
    if prompt:
        @pl.when(t == 0)
        def _():
            ext_ref[0:pre_p, :] = jnp.zeros((pre_p, D_MODEL), F32)
        pos = t * rows + lax.broadcasted_iota(jnp.int32, (rows, 1), 0)
        inv_cnt = [1.0 / jnp.minimum(pos + 1, w).astype(F32) for w in POOL_WINDOWS]
    else:
        ext_ref[0:pre_p, :] = ppre_ref[...].reshape(pre_p, D_MODEL)
        inv_cnt = [1.0 / min(PAST_LEN + 1, w) for w in POOL_WINDOWS]

    h = _pool_mixer(h_ref[...].reshape(rows, D_MODEL), nmix_ref[...], ext_ref, wpool_ref,
                    pscale_ref[...], inv_cnt, pre_p, step, rows)
    pstate_ref[...] = ext_ref[pre_p + rows - POOL_BUF * step:pre_p + rows, :].reshape(pstate_ref.shape)
    if prompt:
        ext_ref[0:pre_p, :] = ext_ref[rows:rows + pre_p, :]
    _ffn_ple_tail(h, t, (p_ref, nffn_ref, nple_ref, wup_ref, cw_ref, cb_ref, wdown_ref, wg_ref,
                         wp_ref, cpre_ref, hout_ref, cstate_ref, u_ref, a_ref),
                  pre_c, step, rows, prompt)


def _attn_prompt_layer_kernel(h_ref, p_ref, kprev_ref, kcur_ref, vprev_ref, vcur_ref,
                              nmix_ref, wq_ref, bd_ref, qnw_ref, bias_ref, sink_ref, wo_ref,
                              nffn_ref, nple_ref, wup_ref, cw_ref, cb_ref, wdown_ref, wg_ref, wp_ref,
                              hout_ref, cstate_ref,
                              klo_ref, khi_ref, vlo_ref, vhi_ref, o_ref, u_ref, a_ref,
                              *, rows, pre_c):
    t = pl.program_id(1)
    h = h_ref[...]
    qn = _q_heads(h, nmix_ref[...], wq_ref, bd_ref, qnw_ref[...]).astype(BF16)

    kw = jnp.concatenate([kprev_ref[...], kcur_ref[...]], axis=0)
    vw = jnp.concatenate([vprev_ref[...], vcur_ref[...]], axis=0)
    low = lax.broadcasted_iota(jnp.int32, (1, LANES), 1) < HEAD_DIM
    for pr in range(N_KV // 2):
        lanes = slice(pr * LANES, (pr + 1) * LANES)
        klo_ref[pr] = jnp.where(low, kw[:, lanes], 0.0).astype(BF16)
        khi_ref[pr] = jnp.where(low, 0.0, kw[:, lanes]).astype(BF16)
        vlo_ref[pr] = jnp.where(low, vw[:, lanes], 0.0).astype(BF16)
        vhi_ref[pr] = jnp.where(low, 0.0, vw[:, lanes]).astype(BF16)

    for qb in range(rows // BLOCK):
        qrows = slice(qb * BLOCK, (qb + 1) * BLOCK)
        krows = slice(qb * BLOCK, (qb + 2) * BLOCK)
        table = jnp.where(t == 0, 0, 1) if qb == 0 else 1
        for pr in range(N_KV // 2):
            qs = jnp.concatenate(
                [qn[qrows, g * KV_DIM + pr * LANES:g * KV_DIM + (pr + 1) * LANES] for g in range(GROUP)],
                axis=0)
            kc = jnp.concatenate([klo_ref[pr, krows, :], khi_ref[pr, krows, :]], axis=0)
            s = _dot_nt(qs, kc) + bias_ref[table, pr]
            probs = []
            for half in range(2):
                sh = s[:, half * 2 * BLOCK:(half + 1) * 2 * BLOCK]
                probs.append(_softmax_sink(sh, sink_ref[2 * pr + half]).astype(BF16))
            vc = jnp.concatenate([vlo_ref[pr, krows, :], vhi_ref[pr, krows, :]], axis=0)
            o = _dot(jnp.concatenate(probs, axis=1), vc)
            for g in range(GROUP):
                o_ref[qrows, g * KV_DIM + pr * LANES:g * KV_DIM + (pr + 1) * LANES] = (
                    o[g * BLOCK:(g + 1) * BLOCK, :].astype(BF16))

    h = h + _dot(o_ref[...], wo_ref[...])
    _ffn_ple_tail(h, t, (p_ref, nffn_ref, nple_ref, wup_ref, cw_ref, cb_ref, wdown_ref, wg_ref,
                         wp_ref, None, hout_ref, cstate_ref, u_ref, a_ref),
                  pre_c, 1, rows, True)


def _attn_post_layer_kernel(h_ref, o_ref, p_ref, cpre_ref, wo_ref,
                            nffn_ref, nple_ref, wup_ref, cw_ref, cb_ref, wdown_ref, wg_ref, wp_ref,
                            hout_ref, cstate_ref, u_ref, a_ref, *, rows, step, pre_c):
    h = h_ref[...].reshape(rows, D_MODEL) + _dot(o_ref[...].reshape(rows, D_MODEL).astype(BF16),
                                                 wo_ref[...])
    _ffn_ple_tail(h, 0, (p_ref, nffn_ref, nple_ref, wup_ref, cw_ref, cb_ref, wdown_ref, wg_ref,
                         wp_ref, cpre_ref, hout_ref, cstate_ref, u_ref, a_ref),
                  pre_c, step, rows, False)


def _kv_kernel(h_ref, norm_ref, wk_ref, wv_ref, bd_ref, knw_ref, k_ref, v_ref):
    xn = _rmsnorm(h_ref[...], norm_ref[...]).astype(BF16)
    k = _dot(xn, wk_ref[...])
    ssq = _dot((k * k).astype(BF16), bd_ref[...])
    k_ref[...] = k * lax.rsqrt(ssq * (1.0 / HEAD_DIM) + EPS) * knw_ref[...]
    v_ref[...] = _dot(xn, wv_ref[...])


def _decode_attn_kernel(h_ref, knew_ref, vnew_ref, ck_ref, cv_ref,
                        nmix_ref, wq_ref, bd_ref, qnw_ref, bias_ref, sink_ref, kvmask_ref, rowa_ref,
                        o_ref, qn_ref, kxa_ref, kxb_ref, vxa_ref, vxb_ref, *, kpad):
    c = pl.program_id(0)

    @pl.when(c == 0)
    def _():
        qn_ref[...] = _q_heads(h_ref[...], nmix_ref[...], wq_ref, bd_ref, qnw_ref[...])
        zeros = jnp.zeros((kpad - WINDOW - PAIR_ROWS, KV_DIM), F32)
        for ref in (kxa_ref, kxb_ref, vxa_ref, vxb_ref):
            ref[WINDOW + PAIR_ROWS:kpad, :] = zeros

    kvmask = kvmask_ref[...]
    row_a = rowa_ref[...]
    row_b = 1.0 - row_a
    bias = bias_ref[...]
    sink = sink_ref[...]

    def pair(i, carry):
        r0 = pl.multiple_of((c * (SEQ_CHUNK // 2) + i) * PAIR_ROWS, PAIR_ROWS)
        q8 = qn_ref[pl.ds(r0, PAIR_ROWS), :]
        qg = jnp.concatenate([q8[:, g * KV_DIM:(g + 1) * KV_DIM] for g in range(GROUP)], axis=0)
        qb = (jnp.concatenate([qg] * N_KV, axis=0) * kvmask).astype(BF16)
        knew = knew_ref[pl.ds(r0, PAIR_ROWS), :]
        vnew = vnew_ref[pl.ds(r0, PAIR_ROWS), :]
        for ref, cache, new, j in ((kxa_ref, ck_ref, knew, 0), (kxb_ref, ck_ref, knew, 1),
                                   (vxa_ref, cv_ref, vnew, 0), (vxb_ref, cv_ref, vnew, 1)):
            ref[0:WINDOW, :] = cache[2 * i + j]
            ref[WINDOW:WINDOW + PAIR_ROWS, :] = new
        sa = _dot_nt(qb, kxa_ref[...].astype(BF16))
        sb = _dot_nt(qb, kxb_ref[...].astype(BF16))
        pr = _softmax_sink(sa * row_a + sb * row_b + bias, sink)
        pa = (pr * row_a).astype(BF16)
        pb = (pr * row_b).astype(BF16)
        o = (_dot(pa, vxa_ref[...].astype(BF16)) + _dot(pb, vxb_ref[...].astype(BF16))) * kvmask
        blk = GROUP * PAIR_ROWS
        og = o[0:blk] + o[blk:2 * blk] + o[2 * blk:3 * blk] + o[3 * blk:4 * blk]
        for g in range(GROUP):
            o_ref[pl.ds(r0, PAIR_ROWS), g * KV_DIM:(g + 1) * KV_DIM] = og[g * PAIR_ROWS:(g + 1) * PAIR_ROWS]
        return carry

    lax.fori_loop(0, SEQ_CHUNK // 2, pair, 0)


def _resident(shape):
    zeros = (0,) * len(shape)
    return pl.BlockSpec(shape, lambda *_: zeros, pipeline_mode=pl.Buffered(1))


def _params(n_axes):
    return pltpu.CompilerParams(dimension_semantics=("arbitrary",) * n_axes,
                                vmem_limit_bytes=VMEM_LIMIT_BYTES)


def _ffn_weight_specs():
    return [_resident((1, D_MODEL)), _resident((1, D_MODEL)), _resident((D_MODEL, F2)),
            _resident((CONV_W, F2)), _resident((1, F2)), _resident((D_FF, D_MODEL)),
            _resident((D_MODEL, D_MODEL)), _resident((PLE_DIM, D_MODEL))]


def _ffn_weights(W, i):
    return [W["norm_ffn"][i], W["norm_ple"][i], W["w_up"][i], W["conv_w"][i], W["conv_b"][i],
            W["w_down"][i], W["w_ple_gate"][i], W["w_ple_proj"][i]]


def _row_specs(prompt, rows, step):
    if prompt:
        tile = lambda width: pl.BlockSpec((None, rows, width), lambda b, t: (b, t, 0))
        state = lambda r, width: pl.BlockSpec((None, r, width), lambda b, t: (b, 0, 0))
    else:
        tile = lambda width: pl.BlockSpec((rows // step, step, width), lambda b, t: (0, t, 0))
        state = lambda r, width: pl.BlockSpec((r, step, width), lambda b, t: (0, t, 0))
    return tile, state


def _pool_layer(h, p, W, i, *, rows, step, prompt, ppre=None, cpre=None):
    d0, d1, _ = h.shape
    grid = (d0, d1 // rows) if prompt else (1, d1 // step)
    pre_p = 2 * SUBLANES if prompt else POOL_BUF * step
    pre_c = SUBLANES if prompt else CONV_BUF * step
    tile, state = _row_specs(prompt, rows, step)
    in_specs = [tile(D_MODEL), tile(PLE_DIM)]
    args = [h, p]
    if not prompt:
        in_specs += [state(POOL_BUF, D_MODEL), state(CONV_BUF, F2)]
        args += [ppre, cpre]
    in_specs += [_resident((1, D_MODEL)), _resident((len(POOL_WINDOWS), POOL_GROUP, POOL_GROUP)),
                 _resident((1, D_MODEL))] + _ffn_weight_specs()
    args += [W["norm_mix"][i], W["w_pool"][i], W["pool_scale"][i]] + _ffn_weights(W, i)
    kern = functools.partial(_pool_layer_kernel, rows=rows, step=step, pre_p=pre_p, pre_c=pre_c,
                             prompt=prompt)
    state_lead = d0 if prompt else POOL_BUF
    conv_lead = d0 if prompt else CONV_BUF
    return pl.pallas_call(
        kern,
        grid=grid,
        in_specs=in_specs,
        out_specs=[tile(D_MODEL), state(POOL_BUF, D_MODEL), state(CONV_BUF, F2)],
        out_shape=[jax.ShapeDtypeStruct(h.shape, F32),
                   jax.ShapeDtypeStruct((state_lead, POOL_BUF if prompt else d1, D_MODEL), F32),
                   jax.ShapeDtypeStruct((conv_lead, CONV_BUF if prompt else d1, F2), F32)],
        scratch_shapes=[pltpu.VMEM((pre_p + rows, D_MODEL), F32),
                        pltpu.VMEM((pre_c + rows, F2), F32),
                        pltpu.VMEM((rows, D_FF), BF16)],
        compiler_params=_params(2),
        name="pool_layer_prompt" if prompt else "pool_layer_decode",
    )(*args)


def _attn_prompt_layer(h, p, k, v, W, A, i, *, rows):
    n, length, _ = h.shape
    j = i - N_A
    blocks = rows // BLOCK
    tile = lambda width: pl.BlockSpec((None, rows, width), lambda b, t: (b, t, 0))
    prev = pl.BlockSpec((None, BLOCK, KV_DIM), lambda b, t: (b, jnp.maximum(t * blocks - 1, 0), 0))
    in_specs = [tile(D_MODEL), tile(PLE_DIM), prev, tile(KV_DIM), prev, tile(KV_DIM),
                _resident((1, D_MODEL)), _resident((D_MODEL, D_MODEL)), _resident((D_MODEL, D_MODEL)),
                _resident((1, D_MODEL)), _resident((2, N_KV // 2, GROUP * BLOCK, 4 * BLOCK)),
                _resident((N_KV, GROUP * BLOCK, 1)), _resident((D_MODEL, D_MODEL))] + _ffn_weight_specs()
    args = [h, p, k, k, v, v, W["norm_mix"][i], A["w_q"][j], A["bd_q"], A["qnw"][j],
            A["bias_prompt"], A["sink_prompt"][j], A["w_o"][j]] + _ffn_weights(W, i)
    kern = functools.partial(_attn_prompt_layer_kernel, rows=rows, pre_c=SUBLANES)
    half = pltpu.VMEM((N_KV // 2, rows + BLOCK, LANES), BF16)
    return pl.pallas_call(
        kern,
        grid=(n, length // rows),
        in_specs=in_specs,
        out_specs=[tile(D_MODEL), pl.BlockSpec((None, CONV_BUF, F2), lambda b, t: (b, 0, 0))],
        out_shape=[jax.ShapeDtypeStruct((n, length, D_MODEL), F32),
                   jax.ShapeDtypeStruct((n, CONV_BUF, F2), F32)],
        scratch_shapes=[half, half, half, half,
                        pltpu.VMEM((rows, D_MODEL), BF16),
                        pltpu.VMEM((SUBLANES + rows, F2), F32),
                        pltpu.VMEM((rows, D_FF), BF16)],
        compiler_params=_params(2),
        name="attn_layer_prompt",
    )(*args)


def _attn_post_layer(h, o, p, cpre, W, A, i, *, rows, step):
    steps, n_seq, _ = h.shape
    j = i - N_A
    pre_c = CONV_BUF * step
    tile, state = _row_specs(False, rows, step)
    kern = functools.partial(_attn_post_layer_kernel, rows=rows, step=step, pre_c=pre_c)
    return pl.pallas_call(
        kern,
        grid=(1, n_seq // step),
        in_specs=[tile(D_MODEL), tile(D_MODEL), tile(PLE_DIM), state(CONV_BUF, F2),
                  _resident((D_MODEL, D_MODEL))] + _ffn_weight_specs(),
        out_specs=[tile(D_MODEL), state(CONV_BUF, F2)],
        out_shape=[jax.ShapeDtypeStruct(h.shape, F32),
                   jax.ShapeDtypeStruct((CONV_BUF, n_seq, F2), F32)],
        scratch_shapes=[pltpu.VMEM((pre_c + rows, F2), F32), pltpu.VMEM((rows, D_FF), BF16)],
        compiler_params=_params(2),
        name="attn_post_layer_decode",
    )(h, o, p, cpre, A["w_o"][j], *_ffn_weights(W, i))


def _shared_kv(h2d, W, A, *, rows):
    total = h2d.shape[0]
    tile = lambda width: pl.BlockSpec((rows, width), lambda t: (t, 0))
    return pl.pallas_call(
        _kv_kernel,
        grid=(total // rows,),
        in_specs=[tile(D_MODEL), _resident((1, D_MODEL)), _resident((D_MODEL, KV_DIM)),
                  _resident((D_MODEL, KV_DIM)), _resident((KV_DIM, KV_DIM)), _resident((1, KV_DIM))],
        out_specs=[tile(KV_DIM), tile(KV_DIM)],
        out_shape=[jax.ShapeDtypeStruct((total, KV_DIM), F32)] * 2,
        compiler_params=_params(1),
        name="shared_kv",
    )(h2d, W["kv_norm"], A["w_k"], A["w_v"], A["bd_k"], A["knw"])


def _decode_attn(h_bm, knew, vnew, cache_k, cache_v, W, A, i):
    rows = h_bm.shape[0]
    n_seq = cache_k.shape[0]
    j = i - N_A
    kpad = A["bias_decode"].shape[1]
    qrows = N_KV * GROUP * PAIR_ROWS
    full = lambda r, width: pl.BlockSpec((r, width), lambda c: (0, 0))
    cache = pl.BlockSpec((SEQ_CHUNK, WINDOW, KV_DIM), lambda c: (c, 0, 0))
    kern = functools.partial(_decode_attn_kernel, kpad=kpad)
    ext = pltpu.VMEM((kpad, KV_DIM), F32)
    return pl.pallas_call(
        kern,
        grid=(n_seq // SEQ_CHUNK,),
        in_specs=[full(rows, D_MODEL), full(rows, KV_DIM), full(rows, KV_DIM), cache, cache,
                  _resident((1, D_MODEL)), _resident((D_MODEL, D_MODEL)), _resident((D_MODEL, D_MODEL)),
                  _resident((1, D_MODEL)), _resident((qrows, kpad)), _resident((qrows, 1)),
                  _resident((qrows, KV_DIM)), _resident((qrows, 1))],
        out_specs=full(rows, D_MODEL),
        out_shape=jax.ShapeDtypeStruct((rows, D_MODEL), F32),
        scratch_shapes=[pltpu.VMEM((rows, D_MODEL), F32), ext, ext, ext, ext],
        compiler_params=_params(1),
        name="decode_attn",
    )(h_bm, knew, vnew, cache_k, cache_v, W["norm_mix"][i], A["w_q"][j], A["bd_q"], A["qnw"][j],
      A["bias_decode"], A["sink_decode"][j], A["kvmask"], A["row_a"])


def _t5_bucket(d):
    n = jnp.maximum(d, 0)
    max_exact = N_BUCKETS // 2
    nf = jnp.maximum(n, 1).astype(F32)
    large = max_exact + (jnp.log(nf / max_exact) / math.log(MAX_DISTANCE / max_exact)
                         * (N_BUCKETS - max_exact)).astype(jnp.int32)
    large = jnp.minimum(large, N_BUCKETS - 1)
    return jnp.where(n < max_exact, n, large)


def _block_diag_ones(width):
    idx = np.arange(width) // HEAD_DIM
    return jnp.asarray(idx[:, None] == idx[None, :], BF16)


def _attention_tables(rel_bias, sinks, n_seq):
    A = {}
    tab = rel_bias[_t5_bucket(jnp.arange(WINDOW))].astype(F32)
    tab = tab.reshape(WINDOW, N_KV, GROUP)

    q = np.arange(BLOCK)[:, None]
    key = np.arange(2 * BLOCK)[None, :]
    d = q + BLOCK - key
    valid = (d >= 0) & (d < WINDOW)
    dcl = np.clip(d, 0, WINDOW - 1)
    per_head = jnp.where(valid[:, :, None, None], tab[dcl], NEG_INF)
    per_head = per_head.transpose(2, 3, 0, 1)
    per_head = per_head.reshape(N_KV // 2, 2, GROUP, BLOCK, 2 * BLOCK)
    steady = per_head.transpose(0, 2, 3, 1, 4).reshape(N_KV // 2, GROUP * BLOCK, 4 * BLOCK)
    first_mask = np.tile(np.arange(2 * BLOCK) >= BLOCK, 2)[None, None, :]
    A["bias_prompt"] = jnp.stack([jnp.where(first_mask, steady, NEG_INF), steady])

    steps = PAIR_ROWS // 2
    kpad = -(-(WINDOW + PAIR_ROWS) // 16) * 16
    col = np.arange(kpad)[None, :]
    r = np.arange(PAIR_ROWS)[:, None]
    step_of = r % steps
    seq_b = r >= steps
    keyidx = np.where(col < WINDOW, col, WINDOW + (col - WINDOW) % steps)
    own = (col < WINDOW) | ((col < WINDOW + PAIR_ROWS) & (((col - WINDOW) >= steps) == seq_b))
    dd = step_of + WINDOW - keyidx
    dvalid = own & (dd >= 0) & (dd < WINDOW)
    ddc = np.clip(dd, 0, WINDOW - 1)
    dec = jnp.where(dvalid[:, :, None, None], tab[ddc], NEG_INF)
    A["bias_decode"] = dec.transpose(2, 3, 0, 1).reshape(N_KV * GROUP * PAIR_ROWS, kpad)
    A["row_a"] = jnp.asarray(np.tile(~seq_b, (N_KV * GROUP, 1)).reshape(-1, 1), F32)
    lane_kv = np.arange(KV_DIM)[None, :] // HEAD_DIM
    row_kv = np.arange(N_KV * GROUP * PAIR_ROWS)[:, None] // (GROUP * PAIR_ROWS)
    A["kvmask"] = jnp.asarray(lane_kv == row_kv, F32)

    sk = sinks.astype(F32).reshape(-1, N_KV, GROUP)
    A["sink_prompt"] = jnp.repeat(sk, BLOCK, axis=2)[..., None]
    A["sink_decode"] = jnp.repeat(sk.reshape(sk.shape[0], -1), PAIR_ROWS, axis=1)[..., None]
    return A


def _prepare(W):
    A = _attention_tables(W["rel_bias"], W["sinks"], None)
    n_b = W["w_q"].shape[0]
    A["w_q"] = (W["w_q"].reshape(n_b, D_MODEL, N_KV, GROUP, HEAD_DIM).transpose(0, 1, 3, 2, 4)
                .reshape(n_b, D_MODEL, D_MODEL).astype(BF16))
    A["w_o"] = (W["w_o"].reshape(n_b, N_KV, GROUP, HEAD_DIM, D_MODEL).transpose(0, 2, 1, 3, 4)
                .reshape(n_b, D_MODEL, D_MODEL).astype(BF16))
    A["qnw"] = jnp.tile(W["q_norm"], (1, N_HEADS)).reshape(n_b, 1, D_MODEL)
    A["knw"] = jnp.tile(W["k_norm"], N_KV).reshape(1, KV_DIM)
    A["w_k"] = W["w_k"].astype(BF16)
    A["w_v"] = W["w_v"].astype(BF16)
    A["bd_q"] = _block_diag_ones(D_MODEL)
    A["bd_k"] = _block_diag_ones(KV_DIM)
    V = dict(W)
    for name in ("w_pool", "w_up", "w_down", "w_ple_gate", "w_ple_proj"):
        V[name] = W[name].astype(BF16)
    for name in ("norm_mix", "norm_ffn", "norm_ple", "pool_scale", "conv_b"):
        V[name] = W[name][:, None, :]
    V["kv_norm"] = W["kv_norm"][None, :]
    return V, A


def kernel(x_prompt, x_sample, p_prompt, p_sample, state_pool, state_conv, cache_k, cache_v,
           norm_mix, norm_ffn, norm_ple, w_pool, pool_scale, kv_norm, w_k, w_v, k_norm,
           w_q, q_norm, sinks, w_o, rel_bias, w_up, conv_w, conv_b, w_down, w_ple_gate, w_ple_proj):
    W, A = _prepare(dict(
        norm_mix=norm_mix, norm_ffn=norm_ffn, norm_ple=norm_ple, w_pool=w_pool,
        pool_scale=pool_scale, kv_norm=kv_norm, w_k=w_k, w_v=w_v, k_norm=k_norm, w_q=w_q,
        q_norm=q_norm, sinks=sinks, w_o=w_o, rel_bias=rel_bias, w_up=w_up, conv_w=conv_w,
        conv_b=conv_b, w_down=w_down, w_ple_gate=w_ple_gate, w_ple_proj=w_ple_proj))
    batch, seq, _ = x_prompt.shape
    n_seq, steps, _ = x_sample.shape

    h = x_prompt
    pool_p, conv_p = [], []
    for i in range(N_A):
        h, ps, cs = _pool_layer(h, p_prompt[i], W, i, rows=PROMPT_TILE, step=1, prompt=True)
        pool_p.append(ps)
        conv_p.append(cs)
    k_p, v_p = _shared_kv(h.reshape(batch * seq, D_MODEL), W, A, rows=PROMPT_TILE)
    k_p = k_p.reshape(batch, seq, KV_DIM)
    v_p = v_p.reshape(batch, seq, KV_DIM)
    for i in range(N_A, DEPTH):
        h, cs = _attn_prompt_layer(h, p_prompt[i], k_p, v_p, W, A, i, rows=PROMPT_TILE)
        conv_p.append(cs)
    y_prompt = h

    swap = lambda x: x.transpose(1, 0, 2)
    tile_rows = steps * DECODE_SEQ_TILE
    total = steps * n_seq
    hs = swap(x_sample)
    pool_s, conv_s = [], []
    for i in range(N_A):
        hs, ps, cs = _pool_layer(hs, swap(p_sample[i]), W, i, rows=tile_rows, step=DECODE_SEQ_TILE,
                                 prompt=False, ppre=swap(state_pool[i]), cpre=swap(state_conv[i]))
        pool_s.append(swap(ps))
        conv_s.append(swap(cs))
    hs_bm = swap(hs).reshape(total, D_MODEL)
    k_new, v_new = _shared_kv(hs_bm, W, A, rows=total)
    ck = cache_k.reshape(n_seq, WINDOW, KV_DIM)
    cv = cache_v.reshape(n_seq, WINDOW, KV_DIM)
    for i in range(N_A, DEPTH):
        o_bm = _decode_attn(hs_bm, k_new, v_new, ck, cv, W, A, i)
        hs, cs = _attn_post_layer(hs, swap(o_bm.reshape(n_seq, steps, D_MODEL)), swap(p_sample[i]),
                                  swap(state_conv[i]), W, A, i, rows=tile_rows, step=DECODE_SEQ_TILE)
        conv_s.append(swap(cs))
        hs_bm = swap(hs).reshape(total, D_MODEL)
    y_sample = hs_bm.reshape(n_seq, steps, D_MODEL)

    def window(cache, new):
        ext = jnp.concatenate([cache, new.reshape(n_seq, steps, N_KV, HEAD_DIM)], axis=1)
        return ext[:, -WINDOW:]

    kp4 = k_p.reshape(batch, seq, N_KV, HEAD_DIM)[:, -WINDOW:]
    vp4 = v_p.reshape(batch, seq, N_KV, HEAD_DIM)[:, -WINDOW:]
    return (y_prompt, y_sample, jnp.stack(pool_p), jnp.stack(pool_s), jnp.stack(conv_p),
            jnp.stack(conv_s), kp4, window(cache_k, k_new), vp4, window(cache_v, v_new))
```

```python
import functools
import math

import numpy as np
import jax
import jax.numpy as jnp
from jax import lax
from jax.experimental import pallas as pl
from jax.experimental.pallas import tpu as pltpu

D_MODEL = 1024
DEPTH = 4
N_A = DEPTH // 2
POOL_WINDOWS = (2, 4, 8, 16)
POOL_GROUP = D_MODEL // len(POOL_WINDOWS)
POOL_BUF = max(POOL_WINDOWS) - 1
HEAD_DIM = 64
N_HEADS = D_MODEL // HEAD_DIM
N_KV = 4
GROUP = N_HEADS // N_KV
KV_DIM = N_KV * HEAD_DIM
WINDOW = 128
BLOCK = WINDOW
N_BUCKETS = 32
MAX_DISTANCE = 128
D_FF = 2816
F2 = 2 * D_FF
CONV_W = 3
CONV_BUF = CONV_W - 1
PLE_DIM = 256
EPS = 1e-6
PAST_LEN = 8192

LANES = 128
SUBLANES = 8
VMEM_LIMIT_BYTES = 58 * 1024 * 1024

PROMPT_TILE = 256
KV_TILE = 1024
FF_CHUNK = 256
DECODE_SEQ_TILE = 32
SEQ_CHUNK = 16
PAIR_ROWS = 2 * 4
PROMPT_POOL_PRE = 2 * SUBLANES
PROMPT_CONV_PRE = SUBLANES

F32 = jnp.float32
BF16 = jnp.bfloat16
NEG_INF = float("-inf")


def _rmsnorm(x, g):
    ms = jnp.mean(x * x, axis=-1, keepdims=True)
    return x * lax.rsqrt(ms + EPS) * g


def _gelu_tanh(x):
    c = np.float32(math.sqrt(2.0 / math.pi))
    cdf = 0.5 * (1.0 + jnp.tanh(c * (x + 0.044715 * (x * x * x))))
    return x * cdf


def _sigmoid(x):
    return 1.0 / (1.0 + jnp.exp(-x))


def _dot(a, b):
    return jnp.dot(a, b, preferred_element_type=F32)


def _dot_nt(a, b):
    return lax.dot_general(a, b, (((1,), (1,)), ((), ())), preferred_element_type=F32)


def _pool_mixer(h, norm_g, pcarry_ref, pstate_ref, wpool_ref, scale, inv_cnt, pre, step, rows, carry):
    xn = _rmsnorm(h, norm_g)
    ext = jnp.concatenate([pcarry_ref[...], xn], axis=0)
    pstate_ref[...] = ext[pre + rows - POOL_BUF * step:pre + rows, :].reshape(pstate_ref.shape)
    if carry:
        pcarry_ref[...] = ext[rows:rows + pre, :]
    outs = []
    for g, w in enumerate(POOL_WINDOWS):
        cols = slice(g * POOL_GROUP, (g + 1) * POOL_GROUP)
        s = ext[:, cols]
        width = 1
        while width < w:
            s = s + pltpu.roll(s, width * step, 0)
            width *= 2
        cur = xn[:, cols]
        d = s[pre:pre + rows, :] * inv_cnt[g] - cur
        outs.append(_dot(d.astype(BF16), wpool_ref[g]))
    return h + jnp.concatenate(outs, axis=1) * scale


def _conv_ffn(h, norm_g, ucarry_ref, wup_ref, cw_ref, cb_ref, wdown_ref, cstate_ref,
              pre, step, rows):
    xn = _rmsnorm(h, norm_g).astype(BF16)
    n_chunks = D_FF // FF_CHUNK
    halves = lambda c: [slice(base + c * FF_CHUNK, base + (c + 1) * FF_CHUNK) for base in (0, D_FF)]
    up = lambda c: [_dot(xn, wup_ref[:, cols]) for cols in halves(c)]

    f = None
    nxt = up(0)
    for c in range(n_chunks):
        us = nxt
        if c + 1 < n_chunks:
            nxt = up(c + 1)
        conv = []
        for cols, u in zip(halves(c), us):
            ue = jnp.concatenate([ucarry_ref[:, cols], u], axis=0)
            ucarry_ref[:, cols] = u[rows - pre:rows, :]
            acc = cb_ref[:, cols] + pltpu.roll(ue, 2 * step, 0)[pre:pre + rows, :] * cw_ref[0:1, cols]
            acc = acc + pltpu.roll(ue, step, 0)[pre:pre + rows, :] * cw_ref[1:2, cols]
            conv.append(acc + u * cw_ref[2:3, cols])
        a = (_gelu_tanh(conv[0]) * conv[1]).astype(BF16)
        part = _dot(a, wdown_ref[c * FF_CHUNK:(c + 1) * FF_CHUNK, :])
        f = part if f is None else f + part
    cstate_ref[...] = ucarry_ref[pre - CONV_BUF * step:pre, :].reshape(cstate_ref.shape)
    return h + f


def _ple(h, norm_g, p, wg_ref, wp_ref):
    xn = _rmsnorm(h, norm_g).astype(BF16)
    gate = _sigmoid(_dot(xn, wg_ref[...]))
    return h + gate * _dot(p.astype(BF16), wp_ref[...])


def _q_heads(h, norm_g, wq_ref, bd_ref, qnw):
    xn = _rmsnorm(h, norm_g).astype(BF16)
    q = _dot(xn, wq_ref[...])
    ssq = _dot((q * q).astype(BF16), bd_ref[...])
    return q * lax.rsqrt(ssq * (1.0 / HEAD_DIM) + EPS) * (qnw * HEAD_DIM ** -0.5)


def _ffn_ple_tail(h, t, refs, pre_c, step, rows, prompt):
    (p_ref, nffn_ref, nple_ref, wup_ref, cw_ref, cb_ref, wdown_ref, wg_ref, wp_ref,
     cpre_ref, hout_ref, cstate_ref, ucarry_ref) = refs
    if prompt:
        @pl.when(t == 0)
        def _():
            ucarry_ref[...] = jnp.zeros((pre_c, F2), F32)
    else:
        ucarry_ref[...] = cpre_ref[...].reshape(pre_c, F2)
    h = _conv_ffn(h, nffn_ref[...], ucarry_ref, wup_ref, cw_ref, cb_ref, wdown_ref,
                  cstate_ref, pre_c, step, rows)
    p = p_ref[...].reshape(rows, PLE_DIM)
    hout_ref[...] = _ple(h, nple_ref[...], p, wg_ref, wp_ref).reshape(hout_ref.shape)


def _pool_layer_kernel(*refs, rows, step, pre_p, pre_c, prompt):
    it = iter(refs)
    h_ref, p_ref = next(it), next(it)
    ppre_ref = None if prompt else next(it)
    cpre_ref = None if prompt else next(it)
    nmix_ref, wpool_ref, pscale_ref = next(it), next(it), next(it)
    nffn_ref, nple_ref, wup_ref, cw_ref, cb_ref, wdown_ref, wg_ref, wp_ref = (next(it) for _ in range(8))
    hout_ref, pstate_ref, cstate_ref = next(it), next(it), next(it)
    pcarry_ref, ucarry_ref = next(it), next(it)

    t = pl.program_id(1)
    if prompt:
        @pl.when(t == 0)
        def _():
            pcarry_ref[...] = jnp.zeros((pre_p, D_MODEL), F32)
        pos = t * rows + lax.broadcasted_iota(jnp.int32, (rows, POOL_GROUP), 0)
        inv_cnt = [1.0 / jnp.minimum(pos + 1, w).astype(F32) for w in POOL_WINDOWS]
    else:
        pcarry_ref[...] = ppre_ref[...].reshape(pre_p, D_MODEL)
        inv_cnt = [1.0 / min(PAST_LEN + 1, w) for w in POOL_WINDOWS]

    h = _pool_mixer(h_ref[...].reshape(rows, D_MODEL), nmix_ref[...], pcarry_ref, pstate_ref,
                    wpool_ref, pscale_ref[...], inv_cnt, pre_p, step, rows, carry=prompt)
    _ffn_ple_tail(h, t, (p_ref, nffn_ref, nple_ref, wup_ref, cw_ref, cb_ref, wdown_ref, wg_ref,
                         wp_ref, cpre_ref, hout_ref, cstate_ref, ucarry_ref),
                  pre_c, step, rows, prompt)


def _attn_prompt_layer_kernel(h_ref, p_ref, kprev_ref, kcur_ref, vprev_ref, vcur_ref,
                              nmix_ref, wq_ref, bd_ref, qnw_ref, bias_ref, sink_ref, wo_ref,
                              nffn_ref, nple_ref, wup_ref, cw_ref, cb_ref, wdown_ref, wg_ref, wp_ref,
                              hout_ref, cstate_ref,
                              klo_ref, khi_ref, vlo_ref, vhi_ref, o_ref, ucarry_ref,
                              *, rows, pre_c):
    t = pl.program_id(1)
    h = h_ref[...]
    qn = _q_heads(h, nmix_ref[...], wq_ref, bd_ref, qnw_ref[...]).astype(BF16)

    kw = jnp.concatenate([kprev_ref[...], kcur_ref[...]], axis=0)
    vwt = jnp.concatenate([vprev_ref[...], vcur_ref[...]], axis=0).T
    low = lax.broadcasted_iota(jnp.int32, (1, LANES), 1) < HEAD_DIM
    top = lax.broadcasted_iota(jnp.int32, (LANES, 1), 0) < HEAD_DIM
    ones_top = jnp.broadcast_to(jnp.where(top, 1.0, 0.0), (LANES, rows + BLOCK))
    for pr in range(N_KV // 2):
        lanes = slice(pr * LANES, (pr + 1) * LANES)
        klo_ref[pr] = jnp.where(low, kw[:, lanes], 0.0).astype(BF16)
        khi_ref[pr] = jnp.where(low, 0.0, kw[:, lanes]).astype(BF16)
        vlo_ref[pr] = jnp.concatenate([jnp.where(top, vwt[lanes, :], 0.0), ones_top], axis=0).astype(BF16)
        vhi_ref[pr] = jnp.concatenate([jnp.where(top, 0.0, vwt[lanes, :]), 1.0 - ones_top],
                                      axis=0).astype(BF16)

    def scores(qb, pr):
        qrows = slice(qb * BLOCK, (qb + 1) * BLOCK)
        krows = slice(qb * BLOCK, (qb + 2) * BLOCK)
        qs = jnp.concatenate(
            [qn[qrows, g * KV_DIM + pr * LANES:g * KV_DIM + (pr + 1) * LANES] for g in range(GROUP)],
            axis=0)
        kc = jnp.concatenate([klo_ref[pr, krows, :], khi_ref[pr, krows, :]], axis=0)
        return _dot_nt(kc, qs)

    units = [(qb, pr) for qb in range(rows // BLOCK) for pr in range(N_KV // 2)]
    nxt = scores(*units[0])
    for n, (qb, pr) in enumerate(units):
        s = nxt
        if n + 1 < len(units):
            nxt = scores(*units[n + 1])
        qrows = slice(qb * BLOCK, (qb + 1) * BLOCK)
        krows = slice(qb * BLOCK, (qb + 2) * BLOCK)
        table = jnp.where(t == 0, 0, 1) if qb == 0 else 1
        s = s + bias_ref[table, pr]
        es, sinkterms = [], []
        for half in range(2):
            sh = s[half * 2 * BLOCK:(half + 1) * 2 * BLOCK, :]
            sink = sink_ref[2 * pr + half]
            m = jnp.maximum(jnp.max(sh, axis=0, keepdims=True), sink)
            es.append(jnp.exp(sh - m).astype(BF16))
            sinkterms.append(jnp.exp(sink - m))
        vct = jnp.concatenate([vlo_ref[pr, :, krows], vhi_ref[pr, :, krows]], axis=1)
        ov = _dot(vct, jnp.concatenate(es, axis=0))
        denom = ov[LANES:, :] + jnp.where(top, sinkterms[0], sinkterms[1])
        o = (ov[:LANES, :] * (1.0 / denom)).T
        for g in range(GROUP):
            o_ref[qrows, g * KV_DIM + pr * LANES:g * KV_DIM + (pr + 1) * LANES] = (
                o[g * BLOCK:(g + 1) * BLOCK, :].astype(BF16))

    h = h + _dot(o_ref[...], wo_ref[...])
    _ffn_ple_tail(h, t, (p_ref, nffn_ref, nple_ref, wup_ref, cw_ref, cb_ref, wdown_ref, wg_ref,
                         wp_ref, None, hout_ref, cstate_ref, ucarry_ref),
                  pre_c, 1, rows, True)


def _attn_post_layer_kernel(h_ref, o_ref, p_ref, cpre_ref, wo_ref,
                            nffn_ref, nple_ref, wup_ref, cw_ref, cb_ref, wdown_ref, wg_ref, wp_ref,
                            hout_ref, cstate_ref, ucarry_ref, *, rows, step, pre_c):
    h = h_ref[...].reshape(rows, D_MODEL) + _dot(o_ref[...].reshape(rows, D_MODEL).astype(BF16),
                                                 wo_ref[...])
    _ffn_ple_tail(h, 0, (p_ref, nffn_ref, nple_ref, wup_ref, cw_ref, cb_ref, wdown_ref, wg_ref,
                         wp_ref, cpre_ref, hout_ref, cstate_ref, ucarry_ref),
                  pre_c, step, rows, False)


def _kv_kernel(h_ref, norm_ref, wk_ref, wv_ref, bd_ref, knw_ref, k_ref, v_ref):
    xn = _rmsnorm(h_ref[...], norm_ref[...]).astype(BF16)
    k = _dot(xn, wk_ref[...])
    ssq = _dot((k * k).astype(BF16), bd_ref[...])
    k_ref[...] = k * lax.rsqrt(ssq * (1.0 / HEAD_DIM) + EPS) * knw_ref[...]
    v_ref[...] = _dot(xn, wv_ref[...])


def _softmax_sink(s, sink):
    m = jnp.maximum(jnp.max(s, axis=-1, keepdims=True), sink)
    e = jnp.exp(s - m)
    denom = jnp.sum(e, axis=-1, keepdims=True) + jnp.exp(sink - m)
    return e * (1.0 / denom)


def _decode_attn_kernel(h_ref, knew_ref, vnew_ref, ck_ref, cv_ref,
                        nmix_ref, wq_ref, bd_ref, qnw_ref, bias_ref, sink_ref, kvmask_ref, rowa_ref,
                        o_ref, qn_ref, kxa_ref, kxb_ref, vxa_ref, vxb_ref, *, kpad):
    c = pl.program_id(0)

    @pl.when(c == 0)
    def _():
        qn_ref[...] = _q_heads(h_ref[...], nmix_ref[...], wq_ref, bd_ref, qnw_ref[...])
        zeros = jnp.zeros((kpad - WINDOW - PAIR_ROWS, KV_DIM), F32)
        for ref in (kxa_ref, kxb_ref, vxa_ref, vxb_ref):
            ref[WINDOW + PAIR_ROWS:kpad, :] = zeros

    kvmask = kvmask_ref[...]
    row_a = rowa_ref[...]
    row_b = 1.0 - row_a
    bias = bias_ref[...]
    sink = sink_ref[...]

    def pair(i, carry):
        r0 = pl.multiple_of((c * (SEQ_CHUNK // 2) + i) * PAIR_ROWS, PAIR_ROWS)
        q8 = qn_ref[pl.ds(r0, PAIR_ROWS), :]
        qg = jnp.concatenate([q8[:, g * KV_DIM:(g + 1) * KV_DIM] for g in range(GROUP)], axis=0)
        qb = (jnp.concatenate([qg] * N_KV, axis=0) * kvmask).astype(BF16)
        knew = knew_ref[pl.ds(r0, PAIR_ROWS), :]
        vnew = vnew_ref[pl.ds(r0, PAIR_ROWS), :]
        for ref, cache, new, j in ((kxa_ref, ck_ref, knew, 0), (kxb_ref, ck_ref, knew, 1),
                                   (vxa_ref, cv_ref, vnew, 0), (vxb_ref, cv_ref, vnew, 1)):
            ref[0:WINDOW, :] = cache[2 * i + j]
            ref[WINDOW:WINDOW + PAIR_ROWS, :] = new
        sa = _dot_nt(qb, kxa_ref[...].astype(BF16))
        sb = _dot_nt(qb, kxb_ref[...].astype(BF16))
        pr = _softmax_sink(sa * row_a + sb * row_b + bias, sink)
        pa = (pr * row_a).astype(BF16)
        pb = (pr * row_b).astype(BF16)
        o = (_dot(pa, vxa_ref[...].astype(BF16)) + _dot(pb, vxb_ref[...].astype(BF16))) * kvmask
        blk = GROUP * PAIR_ROWS
        og = o[0:blk] + o[blk:2 * blk] + o[2 * blk:3 * blk] + o[3 * blk:4 * blk]
        for g in range(GROUP):
            o_ref[pl.ds(r0, PAIR_ROWS), g * KV_DIM:(g + 1) * KV_DIM] = og[g * PAIR_ROWS:(g + 1) * PAIR_ROWS]
        return carry

    lax.fori_loop(0, SEQ_CHUNK // 2, pair, 0)


def _resident(shape, layer=None):
    zeros = (0,) * len(shape)
    if layer is None:
        return pl.BlockSpec(shape, lambda *_: zeros, pipeline_mode=pl.Buffered(1))
    return pl.BlockSpec((None,) + shape, lambda *_: (layer,) + zeros, pipeline_mode=pl.Buffered(1))


def _params(n_axes):
    return pltpu.CompilerParams(dimension_semantics=("arbitrary",) * n_axes,
                                vmem_limit_bytes=VMEM_LIMIT_BYTES)


def _ffn_weight_specs(i):
    return [_resident((1, D_MODEL), i), _resident((1, D_MODEL), i), _resident((D_MODEL, F2), i),
            _resident((CONV_W, F2), i), _resident((1, F2), i), _resident((D_FF, D_MODEL), i),
            _resident((D_MODEL, D_MODEL), i), _resident((PLE_DIM, D_MODEL), i)]


def _ffn_weights(W):
    return [W["norm_ffn"], W["norm_ple"], W["w_up"], W["conv_w"], W["conv_b"], W["w_down"],
            W["w_ple_gate"], W["w_ple_proj"]]


def _ffn_scratch(pre_c):
    return [pltpu.VMEM((pre_c, F2), F32)]


def _row_specs(prompt, rows, step):
    def spec(block, index, layer):
        if layer is None:
            return pl.BlockSpec(block, index)
        return pl.BlockSpec((None,) + block, lambda b, t: (layer,) + index(b, t))

    if prompt:
        tile = lambda width, layer=None: spec((None, rows, width), lambda b, t: (b, t, 0), layer)
        state = lambda r, width, layer=None: spec((None, r, width), lambda b, t: (b, 0, 0), layer)
    else:
        tile = lambda width, layer=None: spec((rows // step, step, width), lambda b, t: (0, t, 0), layer)
        state = lambda r, width, layer=None: spec((r, step, width), lambda b, t: (0, t, 0), layer)
    return tile, state


def _pool_layer(h, p, W, i, *, rows, step, prompt, ppre=None, cpre=None):
    d0, d1, _ = h.shape
    grid = (d0, d1 // rows) if prompt else (1, d1 // step)
    pre_p = PROMPT_POOL_PRE if prompt else POOL_BUF * step
    pre_c = PROMPT_CONV_PRE if prompt else CONV_BUF * step
    tile, state = _row_specs(prompt, rows, step)
    in_specs = [tile(D_MODEL), tile(PLE_DIM, i)]
    args = [h, p]
    if not prompt:
        in_specs += [state(POOL_BUF, D_MODEL, i), state(CONV_BUF, F2, i)]
        args += [ppre, cpre]
    in_specs += [_resident((1, D_MODEL), i),
                 _resident((len(POOL_WINDOWS), POOL_GROUP, POOL_GROUP), i),
                 _resident((1, D_MODEL), i)] + _ffn_weight_specs(i)
    args += [W["norm_mix"], W["w_pool"], W["pool_scale"]] + _ffn_weights(W)
    kern = functools.partial(_pool_layer_kernel, rows=rows, step=step, pre_p=pre_p, pre_c=pre_c,
                             prompt=prompt)
    state_lead = d0 if prompt else POOL_BUF
    conv_lead = d0 if prompt else CONV_BUF
    return pl.pallas_call(
        kern,
        grid=grid,
        in_specs=in_specs,
        out_specs=[tile(D_MODEL), state(POOL_BUF, D_MODEL), state(CONV_BUF, F2)],
        out_shape=[jax.ShapeDtypeStruct(h.shape, F32),
                   jax.ShapeDtypeStruct((state_lead, POOL_BUF if prompt else d1, D_MODEL), F32),
                   jax.ShapeDtypeStruct((conv_lead, CONV_BUF if prompt else d1, F2), F32)],
        scratch_shapes=[pltpu.VMEM((pre_p, D_MODEL), F32)] + _ffn_scratch(pre_c),
        compiler_params=_params(2),
        name="pool_layer_prompt" if prompt else "pool_layer_decode",
    )(*args)


def _attn_prompt_layer(h, p, k, v, W, A, i, *, rows):
    n, length, _ = h.shape
    j = i - N_A
    blocks = rows // BLOCK
    tile, _ = _row_specs(True, rows, 1)
    prev = pl.BlockSpec((None, BLOCK, KV_DIM), lambda b, t: (b, jnp.maximum(t * blocks - 1, 0), 0))
    in_specs = [tile(D_MODEL), tile(PLE_DIM, i), prev, tile(KV_DIM), prev, tile(KV_DIM),
                _resident((1, D_MODEL), i), _resident((D_MODEL, D_MODEL), j),
                _resident((D_MODEL, D_MODEL)), _resident((1, D_MODEL), j),
                _resident((2, N_KV // 2, 4 * BLOCK, GROUP * BLOCK)),
                _resident((N_KV, 1, GROUP * BLOCK), j), _resident((D_MODEL, D_MODEL), j)]
    in_specs += _ffn_weight_specs(i)
    args = [h, p, k, k, v, v, W["norm_mix"], A["w_q"], A["bd_q"], A["qnw"],
            A["bias_prompt"], A["sink_prompt"], A["w_o"]] + _ffn_weights(W)
    kern = functools.partial(_attn_prompt_layer_kernel, rows=rows, pre_c=PROMPT_CONV_PRE)
    khalf = pltpu.VMEM((N_KV // 2, rows + BLOCK, LANES), BF16)
    vhalf = pltpu.VMEM((N_KV // 2, 2 * LANES, rows + BLOCK), BF16)
    return pl.pallas_call(
        kern,
        grid=(n, length // rows),
        in_specs=in_specs,
        out_specs=[tile(D_MODEL), pl.BlockSpec((None, CONV_BUF, F2), lambda b, t: (b, 0, 0))],
        out_shape=[jax.ShapeDtypeStruct((n, length, D_MODEL), F32),
                   jax.ShapeDtypeStruct((n, CONV_BUF, F2), F32)],
        scratch_shapes=[khalf, khalf, vhalf, vhalf, pltpu.VMEM((rows, D_MODEL), BF16)]
        + _ffn_scratch(PROMPT_CONV_PRE),
        compiler_params=_params(2),
        name="attn_layer_prompt",
    )(*args)


def _attn_post_layer(h, o, p, cpre, W, A, i, *, rows, step):
    steps, n_seq, _ = h.shape
    j = i - N_A
    pre_c = CONV_BUF * step
    tile, state = _row_specs(False, rows, step)
    kern = functools.partial(_attn_post_layer_kernel, rows=rows, step=step, pre_c=pre_c)
    return pl.pallas_call(
        kern,
        grid=(1, n_seq // step),
        in_specs=[tile(D_MODEL), tile(D_MODEL), tile(PLE_DIM, i), state(CONV_BUF, F2, i),
                  _resident((D_MODEL, D_MODEL), j)] + _ffn_weight_specs(i),
        out_specs=[tile(D_MODEL), state(CONV_BUF, F2)],
        out_shape=[jax.ShapeDtypeStruct(h.shape, F32),
                   jax.ShapeDtypeStruct((CONV_BUF, n_seq, F2), F32)],
        scratch_shapes=_ffn_scratch(pre_c),
        compiler_params=_params(2),
        name="attn_post_layer_decode",
    )(h, o, p, cpre, A["w_o"], *_ffn_weights(W))


def _shared_kv(h2d, W, A, *, rows):
    total = h2d.shape[0]
    tile = lambda width: pl.BlockSpec((rows, width), lambda t: (t, 0))
    return pl.pallas_call(
        _kv_kernel,
        grid=(total // rows,),
        in_specs=[tile(D_MODEL), _resident((1, D_MODEL)), _resident((D_MODEL, KV_DIM)),
                  _resident((D_MODEL, KV_DIM)), _resident((KV_DIM, KV_DIM)), _resident((1, KV_DIM))],
        out_specs=[tile(KV_DIM), tile(KV_DIM)],
        out_shape=[jax.ShapeDtypeStruct((total, KV_DIM), F32)] * 2,
        compiler_params=_params(1),
        name="shared_kv",
    )(h2d, W["kv_norm"], A["w_k"], A["w_v"], A["bd_k"], A["knw"])


def _decode_attn(h_bm, knew, vnew, cache_k, cache_v, W, A, i):
    rows = h_bm.shape[0]
    n_seq = cache_k.shape[0]
    j = i - N_A
    kpad = A["bias_decode"].shape[1]
    qrows = N_KV * GROUP * PAIR_ROWS
    full = lambda r, width: pl.BlockSpec((r, width), lambda c: (0, 0))
    cache = pl.BlockSpec((SEQ_CHUNK, WINDOW, KV_DIM), lambda c: (c, 0, 0))
    kern = functools.partial(_decode_attn_kernel, kpad=kpad)
    ext = pltpu.VMEM((kpad, KV_DIM), F32)
    return pl.pallas_call(
        kern,
        grid=(n_seq // SEQ_CHUNK,),
        in_specs=[full(rows, D_MODEL), full(rows, KV_DIM), full(rows, KV_DIM), cache, cache,
                  _resident((1, D_MODEL), i), _resident((D_MODEL, D_MODEL), j),
                  _resident((D_MODEL, D_MODEL)), _resident((1, D_MODEL), j),
                  _resident((qrows, kpad)), _resident((qrows, 1), j),
                  _resident((qrows, KV_DIM)), _resident((qrows, 1))],
        out_specs=full(rows, D_MODEL),
        out_shape=jax.ShapeDtypeStruct((rows, D_MODEL), F32),
        scratch_shapes=[pltpu.VMEM((rows, D_MODEL), F32), ext, ext, ext, ext],
        compiler_params=_params(1),
        name="decode_attn",
    )(h_bm, knew, vnew, cache_k, cache_v, W["norm_mix"], A["w_q"], A["bd_q"], A["qnw"],
      A["bias_decode"], A["sink_decode"], A["kvmask"], A["row_a"])


def _t5_bucket(d):
    n = jnp.maximum(d, 0)
    max_exact = N_BUCKETS // 2
    nf = jnp.maximum(n, 1).astype(F32)
    large = max_exact + (jnp.log(nf / max_exact) / math.log(MAX_DISTANCE / max_exact)
                         * (N_BUCKETS - max_exact)).astype(jnp.int32)
    large = jnp.minimum(large, N_BUCKETS - 1)
    return jnp.where(n < max_exact, n, large)


def _block_diag_ones(width):
    idx = np.arange(width) // HEAD_DIM
    return jnp.asarray(idx[:, None] == idx[None, :], BF16)


def _banded_bias(tab):
    heads = tab.shape[1]
    period = 3 * BLOCK + 1
    base = jnp.concatenate([jnp.full((heads, BLOCK), NEG_INF, F32), tab[::-1].T,
                            jnp.full((heads, period - 2 * BLOCK), NEG_INF, F32)], axis=1)
    flat = jnp.tile(base, (1, BLOCK))[:, :BLOCK * (period - 1)]
    skew = flat.reshape(heads, BLOCK, period - 1)
    return skew[:, :, BLOCK - 1:3 * BLOCK - 1]


def _attention_tables(rel_bias, sinks):
    A = {}
    tab = rel_bias[_t5_bucket(jnp.arange(WINDOW))].astype(F32)

    per_head = _banded_bias(tab).reshape(N_KV // 2, 2, GROUP, BLOCK, 2 * BLOCK)
    steady = per_head.transpose(0, 1, 4, 2, 3).reshape(N_KV // 2, 4 * BLOCK, GROUP * BLOCK)
    first_mask = np.tile(np.arange(2 * BLOCK) >= BLOCK, 2)[None, :, None]
    A["bias_prompt"] = jnp.stack([jnp.where(first_mask, steady, NEG_INF), steady])

    tab3 = tab.reshape(WINDOW, N_KV, GROUP)
    steps = PAIR_ROWS // 2
    kpad = -(-(WINDOW + PAIR_ROWS) // 16) * 16
    col = np.arange(kpad)[None, :]
    r = np.arange(PAIR_ROWS)[:, None]
    step_of = r % steps
    seq_b = r >= steps
    keyidx = np.where(col < WINDOW, col, WINDOW + (col - WINDOW) % steps)
    own = (col < WINDOW) | ((col < WINDOW + PAIR_ROWS) & (((col - WINDOW) >= steps) == seq_b))
    dd = step_of + WINDOW - keyidx
    dvalid = own & (dd >= 0) & (dd < WINDOW)
    ddc = np.clip(dd, 0, WINDOW - 1)
    dec = jnp.where(dvalid[:, :, None, None], tab3[ddc], NEG_INF)
    A["bias_decode"] = dec.transpose(2, 3, 0, 1).reshape(N_KV * GROUP * PAIR_ROWS, kpad)
    A["row_a"] = jnp.asarray(np.tile(~seq_b, (N_KV * GROUP, 1)).reshape(-1, 1), F32)
    lane_kv = np.arange(KV_DIM)[None, :] // HEAD_DIM
    row_kv = np.arange(N_KV * GROUP * PAIR_ROWS)[:, None] // (GROUP * PAIR_ROWS)
    A["kvmask"] = jnp.asarray(lane_kv == row_kv, F32)

    sk = sinks.astype(F32).reshape(-1, N_KV, GROUP)
    A["sink_prompt"] = jnp.repeat(sk, BLOCK, axis=2)[:, :, None, :]
    A["sink_decode"] = jnp.repeat(sk.reshape(sk.shape[0], -1), PAIR_ROWS, axis=1)[..., None]
    return A


def _prepare(W):
    A = _attention_tables(W["rel_bias"], W["sinks"])
    n_b = W["w_q"].shape[0]
    A["w_q"] = (W["w_q"].reshape(n_b, D_MODEL, N_KV, GROUP, HEAD_DIM).transpose(0, 1, 3, 2, 4)
                .reshape(n_b, D_MODEL, D_MODEL).astype(BF16))
    A["w_o"] = (W["w_o"].reshape(n_b, N_KV, GROUP, HEAD_DIM, D_MODEL).transpose(0, 2, 1, 3, 4)
                .reshape(n_b, D_MODEL, D_MODEL).astype(BF16))
    A["qnw"] = jnp.tile(W["q_norm"], (1, N_HEADS)).reshape(n_b, 1, D_MODEL)
    A["knw"] = jnp.tile(W["k_norm"], N_KV).reshape(1, KV_DIM)
    A["w_k"] = W["w_k"].astype(BF16)
    A["w_v"] = W["w_v"].astype(BF16)
    A["bd_q"] = _block_diag_ones(D_MODEL)
    A["bd_k"] = _block_diag_ones(KV_DIM)
    V = dict(W)
    for name in ("w_pool", "w_up", "w_down", "w_ple_gate", "w_ple_proj"):
        V[name] = W[name].astype(BF16)
    for name in ("norm_mix", "norm_ffn", "norm_ple", "pool_scale", "conv_b"):
        V[name] = W[name][:, None, :]
    V["kv_norm"] = W["kv_norm"][None, :]
    return V, A


def kernel(x_prompt, x_sample, p_prompt, p_sample, state_pool, state_conv, cache_k, cache_v,
           norm_mix, norm_ffn, norm_ple, w_pool, pool_scale, kv_norm, w_k, w_v, k_norm,
           w_q, q_norm, sinks, w_o, rel_bias, w_up, conv_w, conv_b, w_down, w_ple_gate, w_ple_proj):
    W, A = _prepare(dict(
        norm_mix=norm_mix, norm_ffn=norm_ffn, norm_ple=norm_ple, w_pool=w_pool,
        pool_scale=pool_scale, kv_norm=kv_norm, w_k=w_k, w_v=w_v, k_norm=k_norm, w_q=w_q,
        q_norm=q_norm, sinks=sinks, w_o=w_o, rel_bias=rel_bias, w_up=w_up, conv_w=conv_w,
        conv_b=conv_b, w_down=w_down, w_ple_gate=w_ple_gate, w_ple_proj=w_ple_proj))
    batch, seq, _ = x_prompt.shape
    n_seq, steps, _ = x_sample.shape

    h = x_prompt
    pool_p, conv_p = [], []
    for i in range(N_A):
        h, ps, cs = _pool_layer(h, p_prompt, W, i, rows=PROMPT_TILE, step=1, prompt=True)
        pool_p.append(ps)
        conv_p.append(cs)
    k_p, v_p = _shared_kv(h.reshape(batch * seq, D_MODEL), W, A, rows=KV_TILE)
    k_p = k_p.reshape(batch, seq, KV_DIM)
    v_p = v_p.reshape(batch, seq, KV_DIM)
    for i in range(N_A, DEPTH):
        h, cs = _attn_prompt_layer(h, p_prompt, k_p, v_p, W, A, i, rows=PROMPT_TILE)
        conv_p.append(cs)
    y_prompt = h

    swap = lambda x: jnp.swapaxes(x, -3, -2)
    tile_rows = steps * DECODE_SEQ_TILE
    total = steps * n_seq
    hs = swap(x_sample)
    ps_sw, ppre_sw, cpre_sw = swap(p_sample), swap(state_pool), swap(state_conv)
    pool_s, conv_s = [], []
    for i in range(N_A):
        hs, ps, cs = _pool_layer(hs, ps_sw, W, i, rows=tile_rows, step=DECODE_SEQ_TILE,
                                 prompt=False, ppre=ppre_sw, cpre=cpre_sw)
        pool_s.append(ps)
        conv_s.append(cs)
    hs_bm = swap(hs).reshape(total, D_MODEL)
    k_new, v_new = _shared_kv(hs_bm, W, A, rows=total)
    ck = cache_k.reshape(n_seq, WINDOW, KV_DIM)
    cv = cache_v.reshape(n_seq, WINDOW, KV_DIM)
    for i in range(N_A, DEPTH):
        o_bm = _decode_attn(hs_bm, k_new, v_new, ck, cv, W, A, i)
        hs, cs = _attn_post_layer(hs, swap(o_bm.reshape(n_seq, steps, D_MODEL)), ps_sw, cpre_sw,
                                  W, A, i, rows=tile_rows, step=DECODE_SEQ_TILE)
        conv_s.append(cs)
        hs_bm = swap(hs).reshape(total, D_MODEL)
    y_sample = hs_bm.reshape(n_seq, steps, D_MODEL)

    def window(cache, new):
        ext = jnp.concatenate([cache, new.reshape(n_seq, steps, N_KV, HEAD_DIM)], axis=1)
        return ext[:, -WINDOW:]

    kp4 = k_p.reshape(batch, seq, N_KV, HEAD_DIM)[:, -WINDOW:]
    vp4 = v_p.reshape(batch, seq, N_KV, HEAD_DIM)[:, -WINDOW:]
    return (y_prompt, y_sample, jnp.stack(pool_p), swap(jnp.stack(pool_s)), jnp.stack(conv_p),
            swap(jnp.stack(conv_s)), kp4, window(cache_k, k_new), vp4, window(cache_v, v_new))
```

```python
import functools
import math

import numpy as np
import jax
import jax.numpy as jnp
from jax import lax
from jax.experimental import pallas as pl
from jax.experimental.pallas import tpu as pltpu

D_MODEL = 1024
DEPTH = 4
N_A = DEPTH // 2
POOL_WINDOWS = (2, 4, 8, 16)
POOL_GROUP = D_MODEL // len(POOL_WINDOWS)
POOL_BUF = max(POOL_WINDOWS) - 1
HEAD_DIM = 64
N_HEADS = D_MODEL // HEAD_DIM
N_KV = 4
GROUP = N_HEADS // N_KV
KV_DIM = N_KV * HEAD_DIM
WINDOW = 128
BLOCK = WINDOW
N_BUCKETS = 32
MAX_DISTANCE = 128
D_FF = 2816
F2 = 2 * D_FF
CONV_W = 3
CONV_BUF = CONV_W - 1
PLE_DIM = 256
EPS = 1e-6
PAST_LEN = 8192

LANES = 128
SUBLANES = 8
VMEM_LIMIT_BYTES = 58 * 1024 * 1024

PROMPT_TILE = 256
KV_TILE = 1024
FF_CHUNK = 512
UP_AHEAD = 2
DECODE_SEQ_TILE = 32
SEQ_CHUNK = 16
PAIR_ROWS = 2 * 4
PROMPT_POOL_PRE = 2 * SUBLANES
PROMPT_CONV_PRE = SUBLANES

F32 = jnp.float32
BF16 = jnp.bfloat16
NEG_INF = float("-inf")


def _rmsnorm(x, g):
    ms = jnp.mean(x * x, axis=-1, keepdims=True)
    return x * lax.rsqrt(ms + EPS) * g


def _gelu_tanh(x):
    c = np.float32(math.sqrt(2.0 / math.pi))
    cdf = 0.5 * (1.0 + jnp.tanh(c * (x + 0.044715 * (x * x * x))))
    return x * cdf


def _sigmoid(x):
    return 1.0 / (1.0 + jnp.exp(-x))


def _dot(a, b):
    return jnp.dot(a, b, preferred_element_type=F32)


def _dot_nt(a, b):
    return lax.dot_general(a, b, (((1,), (1,)), ((), ())), preferred_element_type=F32)


def _pool_mixer(h, norm_g, pcarry_ref, pstate_ref, wpool_ref, scale, inv_cnt, pre, step, rows, carry):
    xn = _rmsnorm(h, norm_g)
    ext = jnp.concatenate([pcarry_ref[...], xn], axis=0)
    pstate_ref[...] = ext[pre + rows - POOL_BUF * step:pre + rows, :].reshape(pstate_ref.shape)
    if carry:
        pcarry_ref[...] = ext[rows:rows + pre, :]
    outs = []
    for g, w in enumerate(POOL_WINDOWS):
        cols = slice(g * POOL_GROUP, (g + 1) * POOL_GROUP)
        s = ext[:, cols]
        width = 1
        while width < w:
            s = s + pltpu.roll(s, width * step, 0)
            width *= 2
        cur = xn[:, cols]
        d = s[pre:pre + rows, :] * inv_cnt[g] - cur
        outs.append(_dot(d.astype(BF16), wpool_ref[g]))
    return h + jnp.concatenate(outs, axis=1) * scale


def _conv_ffn(h, norm_g, ucarry_ref, ubufs, wup_ref, cw_ref, cb_ref, wdown_ref, cstate_ref,
              pre, step, rows):
    xn = _rmsnorm(h, norm_g).astype(BF16)
    bounds = list(range(0, D_FF, FF_CHUNK)) + [D_FF]
    n_chunks = len(bounds) - 1
    halves = lambda c: [slice(base + bounds[c], base + bounds[c + 1]) for base in (0, D_FF)]
    local = lambda c: [slice(j * FF_CHUNK, j * FF_CHUNK + bounds[c + 1] - bounds[c]) for j in range(2)]

    def up(c):
        ub = ubufs[c % len(ubufs)]
        for cols, loc in zip(halves(c), local(c)):
            ub[0:pre, loc] = ucarry_ref[:, cols]
            ub[pre:pre + rows, loc] = _dot(xn, wup_ref[:, cols])

    for c in range(min(UP_AHEAD, n_chunks)):
        up(c)
    f = None
    for c in range(n_chunks):
        if c + UP_AHEAD < n_chunks:
            up(c + UP_AHEAD)
        ub = ubufs[c % len(ubufs)]
        conv = []
        for cols, loc in zip(halves(c), local(c)):
            ue = ub[:, loc]
            ucarry_ref[:, cols] = ue[rows:rows + pre, :]
            acc = cb_ref[:, cols] + pltpu.roll(ue, 2 * step, 0)[pre:pre + rows, :] * cw_ref[0:1, cols]
            acc = acc + pltpu.roll(ue, step, 0)[pre:pre + rows, :] * cw_ref[1:2, cols]
            conv.append(acc + ue[pre:pre + rows, :] * cw_ref[2:3, cols])
        a = (_gelu_tanh(conv[0]) * conv[1]).astype(BF16)
        part = _dot(a, wdown_ref[bounds[c]:bounds[c + 1], :])
        f = part if f is None else f + part
    cstate_ref[...] = ucarry_ref[pre - CONV_BUF * step:pre, :].reshape(cstate_ref.shape)
    return h + f


def _ple(h, norm_g, p, wg_ref, wp_ref):
    xn = _rmsnorm(h, norm_g).astype(BF16)
    gate = _sigmoid(_dot(xn, wg_ref[...]))
    return h + gate * _dot(p.astype(BF16), wp_ref[...])


def _q_heads(h, norm_g, wq_ref, bd_ref, qnw):
    xn = _rmsnorm(h, norm_g).astype(BF16)
    q = _dot(xn, wq_ref[...])
    ssq = _dot((q * q).astype(BF16), bd_ref[...])
    return q * lax.rsqrt(ssq * (1.0 / HEAD_DIM) + EPS) * (qnw * HEAD_DIM ** -0.5)


def _ffn_ple_tail(h, t, refs, pre_c, step, rows, prompt):
    (p_ref, nffn_ref, nple_ref, wup_ref, cw_ref, cb_ref, wdown_ref, wg_ref, wp_ref,
     cpre_ref, hout_ref, cstate_ref, ucarry_ref, ubufs) = refs
    if prompt:
        @pl.when(t == 0)
        def _():
            ucarry_ref[...] = jnp.zeros((pre_c, F2), F32)
    else:
        ucarry_ref[...] = cpre_ref[...].reshape(pre_c, F2)
    h = _conv_ffn(h, nffn_ref[...], ucarry_ref, ubufs, wup_ref, cw_ref, cb_ref, wdown_ref,
                  cstate_ref, pre_c, step, rows)
    p = p_ref[...].reshape(rows, PLE_DIM)
    hout_ref[...] = _ple(h, nple_ref[...], p, wg_ref, wp_ref).reshape(hout_ref.shape)


def _pool_layer_kernel(*refs, rows, step, pre_p, pre_c, prompt):
    it = iter(refs)
    h_ref, p_ref = next(it), next(it)
    ppre_ref = None if prompt else next(it)
    cpre_ref = None if prompt else next(it)
    nmix_ref, wpool_ref, pscale_ref = next(it), next(it), next(it)
    nffn_ref, nple_ref, wup_ref, cw_ref, cb_ref, wdown_ref, wg_ref, wp_ref = (next(it) for _ in range(8))
    hout_ref, pstate_ref, cstate_ref = next(it), next(it), next(it)
    pcarry_ref, ucarry_ref = next(it), next(it)
    ubufs = list(it)

    t = pl.program_id(1)
    if prompt:
        @pl.when(t == 0)
        def _():
            pcarry_ref[...] = jnp.zeros((pre_p, D_MODEL), F32)
        pos = t * rows + lax.broadcasted_iota(jnp.int32, (rows, POOL_GROUP), 0)
        inv_cnt = [1.0 / jnp.minimum(pos + 1, w).astype(F32) for w in POOL_WINDOWS]
    else:
        pcarry_ref[...] = ppre_ref[...].reshape(pre_p, D_MODEL)
        inv_cnt = [1.0 / min(PAST_LEN + 1, w) for w in POOL_WINDOWS]

    h = _pool_mixer(h_ref[...].reshape(rows, D_MODEL), nmix_ref[...], pcarry_ref, pstate_ref,
                    wpool_ref, pscale_ref[...], inv_cnt, pre_p, step, rows, carry=prompt)
    _ffn_ple_tail(h, t, (p_ref, nffn_ref, nple_ref, wup_ref, cw_ref, cb_ref, wdown_ref, wg_ref,
                         wp_ref, cpre_ref, hout_ref, cstate_ref, ucarry_ref, ubufs),
                  pre_c, step, rows, prompt)


def _attn_prompt_layer_kernel(h_ref, p_ref, kprev_ref, kcur_ref, vprev_ref, vcur_ref,
                              nmix_ref, wq_ref, bd_ref, qnw_ref, bias_ref, sink_ref, wo_ref,
                              nffn_ref, nple_ref, wup_ref, cw_ref, cb_ref, wdown_ref, wg_ref, wp_ref,
                              hout_ref, cstate_ref,
                              klo_ref, khi_ref, vlo_ref, vhi_ref, o_ref, ucarry_ref, *ubufs,
                              rows, pre_c):
    t = pl.program_id(1)
    h = h_ref[...]
    qn = _q_heads(h, nmix_ref[...], wq_ref, bd_ref, qnw_ref[...]).astype(BF16)

    kw = jnp.concatenate([kprev_ref[...], kcur_ref[...]], axis=0)
    vwt = jnp.concatenate([vprev_ref[...], vcur_ref[...]], axis=0).T
    low = lax.broadcasted_iota(jnp.int32, (1, LANES), 1) < HEAD_DIM
    top = lax.broadcasted_iota(jnp.int32, (LANES, 1), 0) < HEAD_DIM
    ones_top = jnp.broadcast_to(jnp.where(top, 1.0, 0.0), (LANES, rows + BLOCK))
    for pr in range(N_KV // 2):
        lanes = slice(pr * LANES, (pr + 1) * LANES)
        klo_ref[pr] = jnp.where(low, kw[:, lanes], 0.0).astype(BF16)
        khi_ref[pr] = jnp.where(low, 0.0, kw[:, lanes]).astype(BF16)
        vlo_ref[pr] = jnp.concatenate([jnp.where(top, vwt[lanes, :], 0.0), ones_top], axis=0).astype(BF16)
        vhi_ref[pr] = jnp.concatenate([jnp.where(top, 0.0, vwt[lanes, :]), 1.0 - ones_top],
                                      axis=0).astype(BF16)

    def scores(qb, pr):
        qrows = slice(qb * BLOCK, (qb + 1) * BLOCK)
        krows = slice(qb * BLOCK, (qb + 2) * BLOCK)
        qs = jnp.concatenate(
            [qn[qrows, g * KV_DIM + pr * LANES:g * KV_DIM + (pr + 1) * LANES] for g in range(GROUP)],
            axis=0)
        kc = jnp.concatenate([klo_ref[pr, krows, :], khi_ref[pr, krows, :]], axis=0)
        return _dot_nt(kc, qs)

    units = [(qb, pr) for qb in range(rows // BLOCK) for pr in range(N_KV // 2)]
    nxt = scores(*units[0])
    for n, (qb, pr) in enumerate(units):
        s = nxt
        if n + 1 < len(units):
            nxt = scores(*units[n + 1])
        qrows = slice(qb * BLOCK, (qb + 1) * BLOCK)
        krows = slice(qb * BLOCK, (qb + 2) * BLOCK)
        table = jnp.where(t == 0, 0, 1) if qb == 0 else 1
        s = s + bias_ref[table, pr]
        es, sinkterms = [], []
        for half in range(2):
            sh = s[half * 2 * BLOCK:(half + 1) * 2 * BLOCK, :]
            sink = sink_ref[2 * pr + half]
            m = jnp.maximum(jnp.max(sh, axis=0, keepdims=True), sink)
            es.append(jnp.exp(sh - m).astype(BF16))
            sinkterms.append(jnp.exp(sink - m))
        vct = jnp.concatenate([vlo_ref[pr, :, krows], vhi_ref[pr, :, krows]], axis=1)
        ov = _dot(vct, jnp.concatenate(es, axis=0))
        denom = ov[LANES:, :] + jnp.where(top, sinkterms[0], sinkterms[1])
        o = (ov[:LANES, :] * (1.0 / denom)).T
        for g in range(GROUP):
            o_ref[qrows, g * KV_DIM + pr * LANES:g * KV_DIM + (pr + 1) * LANES] = (
                o[g * BLOCK:(g + 1) * BLOCK, :].astype(BF16))

    h = h + _dot(o_ref[...], wo_ref[...])
    _ffn_ple_tail(h, t, (p_ref, nffn_ref, nple_ref, wup_ref, cw_ref, cb_ref, wdown_ref, wg_ref,
                         wp_ref, None, hout_ref, cstate_ref, ucarry_ref, list(ubufs)),
                  pre_c, 1, rows, True)


def _attn_post_layer_kernel(h_ref, o_ref, p_ref, cpre_ref, wo_ref,
                            nffn_ref, nple_ref, wup_ref, cw_ref, cb_ref, wdown_ref, wg_ref, wp_ref,
                            hout_ref, cstate_ref, ucarry_ref, *ubufs, rows, step, pre_c):
    h = h_ref[...].reshape(rows, D_MODEL) + _dot(o_ref[...].reshape(rows, D_MODEL).astype(BF16),
                                                 wo_ref[...])
    _ffn_ple_tail(h, 0, (p_ref, nffn_ref, nple_ref, wup_ref, cw_ref, cb_ref, wdown_ref, wg_ref,
                         wp_ref, cpre_ref, hout_ref, cstate_ref, ucarry_ref, list(ubufs)),
                  pre_c, step, rows, False)


def _kv_kernel(h_ref, norm_ref, wk_ref, wv_ref, bd_ref, knw_ref, k_ref, v_ref):
    xn = _rmsnorm(h_ref[...], norm_ref[...]).astype(BF16)
    k = _dot(xn, wk_ref[...])
    ssq = _dot((k * k).astype(BF16), bd_ref[...])
    k_ref[...] = k * lax.rsqrt(ssq * (1.0 / HEAD_DIM) + EPS) * knw_ref[...]
    v_ref[...] = _dot(xn, wv_ref[...])


def _softmax_sink(s, sink):
    m = jnp.maximum(jnp.max(s, axis=-1, keepdims=True), sink)
    e = jnp.exp(s - m)
    denom = jnp.sum(e, axis=-1, keepdims=True) + jnp.exp(sink - m)
    return e * (1.0 / denom)


def _decode_attn_kernel(h_ref, knew_ref, vnew_ref, ck_ref, cv_ref,
                        nmix_ref, wq_ref, bd_ref, qnw_ref, bias_ref, sink_ref, kvmask_ref, rowa_ref,
                        o_ref, qn_ref, kxa_ref, kxb_ref, vxa_ref, vxb_ref, *, kpad):
    c = pl.program_id(0)

    @pl.when(c == 0)
    def _():
        qn_ref[...] = _q_heads(h_ref[...], nmix_ref[...], wq_ref, bd_ref, qnw_ref[...])
        zeros = jnp.zeros((kpad - WINDOW - PAIR_ROWS, KV_DIM), F32)
        for ref in (kxa_ref, kxb_ref, vxa_ref, vxb_ref):
            ref[WINDOW + PAIR_ROWS:kpad, :] = zeros

    kvmask = kvmask_ref[...]
    row_a = rowa_ref[...]
    row_b = 1.0 - row_a
    bias = bias_ref[...]
    sink = sink_ref[...]

    def pair(i, carry):
        r0 = pl.multiple_of((c * (SEQ_CHUNK // 2) + i) * PAIR_ROWS, PAIR_ROWS)
        q8 = qn_ref[pl.ds(r0, PAIR_ROWS), :]
        qg = jnp.concatenate([q8[:, g * KV_DIM:(g + 1) * KV_DIM] for g in range(GROUP)], axis=0)
        qb = (jnp.concatenate([qg] * N_KV, axis=0) * kvmask).astype(BF16)
        knew = knew_ref[pl.ds(r0, PAIR_ROWS), :]
        vnew = vnew_ref[pl.ds(r0, PAIR_ROWS), :]
        for ref, cache, new, j in ((kxa_ref, ck_ref, knew, 0), (kxb_ref, ck_ref, knew, 1),
                                   (vxa_ref, cv_ref, vnew, 0), (vxb_ref, cv_ref, vnew, 1)):
            ref[0:WINDOW, :] = cache[2 * i + j]
            ref[WINDOW:WINDOW + PAIR_ROWS, :] = new
        sa = _dot_nt(qb, kxa_ref[...].astype(BF16))
        sb = _dot_nt(qb, kxb_ref[...].astype(BF16))
        pr = _softmax_sink(sa * row_a + sb * row_b + bias, sink)
        pa = (pr * row_a).astype(BF16)
        pb = (pr * row_b).astype(BF16)
        o = (_dot(pa, vxa_ref[...].astype(BF16)) + _dot(pb, vxb_ref[...].astype(BF16))) * kvmask
        blk = GROUP * PAIR_ROWS
        og = o[0:blk] + o[blk:2 * blk] + o[2 * blk:3 * blk] + o[3 * blk:4 * blk]
        for g in range(GROUP):
            o_ref[pl.ds(r0, PAIR_ROWS), g * KV_DIM:(g + 1) * KV_DIM] = og[g * PAIR_ROWS:(g + 1) * PAIR_ROWS]
        return carry

    lax.fori_loop(0, SEQ_CHUNK // 2, pair, 0)


def _resident(shape, layer=None):
    zeros = (0,) * len(shape)
    if layer is None:
        return pl.BlockSpec(shape, lambda *_: zeros, pipeline_mode=pl.Buffered(1))
    return pl.BlockSpec((None,) + shape, lambda *_: (layer,) + zeros, pipeline_mode=pl.Buffered(1))


def _params(n_axes):
    return pltpu.CompilerParams(dimension_semantics=("arbitrary",) * n_axes,
                                vmem_limit_bytes=VMEM_LIMIT_BYTES)


def _ffn_weight_specs(i):
    return [_resident((1, D_MODEL), i), _resident((1, D_MODEL), i), _resident((D_MODEL, F2), i),
            _resident((CONV_W, F2), i), _resident((1, F2), i), _resident((D_FF, D_MODEL), i),
            _resident((D_MODEL, D_MODEL), i), _resident((PLE_DIM, D_MODEL), i)]


def _ffn_weights(W):
    return [W["norm_ffn"], W["norm_ple"], W["w_up"], W["conv_w"], W["conv_b"], W["w_down"],
            W["w_ple_gate"], W["w_ple_proj"]]


def _ffn_scratch(pre_c, rows):
    return ([pltpu.VMEM((pre_c, F2), F32)]
            + [pltpu.VMEM((pre_c + rows, 2 * FF_CHUNK), F32) for _ in range(UP_AHEAD + 1)])


def _row_specs(prompt, rows, step):
    def spec(block, index, layer):
        if layer is None:
            return pl.BlockSpec(block, index)
        return pl.BlockSpec((None,) + block, lambda b, t: (layer,) + index(b, t))

    if prompt:
        tile = lambda width, layer=None: spec((None, rows, width), lambda b, t: (b, t, 0), layer)
        state = lambda r, width, layer=None: spec((None, r, width), lambda b, t: (b, 0, 0), layer)
    else:
        tile = lambda width, layer=None: spec((rows // step, step, width), lambda b, t: (0, t, 0), layer)
        state = lambda r, width, layer=None: spec((r, step, width), lambda b, t: (0, t, 0), layer)
    return tile, state


def _pool_layer(h, p, W, i, *, rows, step, prompt, ppre=None, cpre=None):
    d0, d1, _ = h.shape
    grid = (d0, d1 // rows) if prompt else (1, d1 // step)
    pre_p = PROMPT_POOL_PRE if prompt else POOL_BUF * step
    pre_c = PROMPT_CONV_PRE if prompt else CONV_BUF * step
    tile, state = _row_specs(prompt, rows, step)
    in_specs = [tile(D_MODEL), tile(PLE_DIM, i)]
    args = [h, p]
    if not prompt:
        in_specs += [state(POOL_BUF, D_MODEL, i), state(CONV_BUF, F2, i)]
        args += [ppre, cpre]
    in_specs += [_resident((1, D_MODEL), i),
                 _resident((len(POOL_WINDOWS), POOL_GROUP, POOL_GROUP), i),
                 _resident((1, D_MODEL), i)] + _ffn_weight_specs(i)
    args += [W["norm_mix"], W["w_pool"], W["pool_scale"]] + _ffn_weights(W)
    kern = functools.partial(_pool_layer_kernel, rows=rows, step=step, pre_p=pre_p, pre_c=pre_c,
                             prompt=prompt)
    state_lead = d0 if prompt else POOL_BUF
    conv_lead = d0 if prompt else CONV_BUF
    return pl.pallas_call(
        kern,
        grid=grid,
        in_specs=in_specs,
        out_specs=[tile(D_MODEL), state(POOL_BUF, D_MODEL), state(CONV_BUF, F2)],
        out_shape=[jax.ShapeDtypeStruct(h.shape, F32),
                   jax.ShapeDtypeStruct((state_lead, POOL_BUF if prompt else d1, D_MODEL), F32),
                   jax.ShapeDtypeStruct((conv_lead, CONV_BUF if prompt else d1, F2), F32)],
        scratch_shapes=[pltpu.VMEM((pre_p, D_MODEL), F32)] + _ffn_scratch(pre_c, rows),
        compiler_params=_params(2),
        name="pool_layer_prompt" if prompt else "pool_layer_decode",
    )(*args)


def _attn_prompt_layer(h, p, k, v, W, A, i, *, rows):
    n, length, _ = h.shape
    j = i - N_A
    blocks = rows // BLOCK
    tile, _ = _row_specs(True, rows, 1)
    prev = pl.BlockSpec((None, BLOCK, KV_DIM), lambda b, t: (b, jnp.maximum(t * blocks - 1, 0), 0))
    in_specs = [tile(D_MODEL), tile(PLE_DIM, i), prev, tile(KV_DIM), prev, tile(KV_DIM),
                _resident((1, D_MODEL), i), _resident((D_MODEL, D_MODEL), j),
                _resident((D_MODEL, D_MODEL)), _resident((1, D_MODEL), j),
                _resident((2, N_KV // 2, 4 * BLOCK, GROUP * BLOCK)),
                _resident((N_KV, 1, GROUP * BLOCK), j), _resident((D_MODEL, D_MODEL), j)]
    in_specs += _ffn_weight_specs(i)
    args = [h, p, k, k, v, v, W["norm_mix"], A["w_q"], A["bd_q"], A["qnw"],
            A["bias_prompt"], A["sink_prompt"], A["w_o"]] + _ffn_weights(W)
    kern = functools.partial(_attn_prompt_layer_kernel, rows=rows, pre_c=PROMPT_CONV_PRE)
    khalf = pltpu.VMEM((N_KV // 2, rows + BLOCK, LANES), BF16)
    vhalf = pltpu.VMEM((N_KV // 2, 2 * LANES, rows + BLOCK), BF16)
    return pl.pallas_call(
        kern,
        grid=(n, length // rows),
        in_specs=in_specs,
        out_specs=[tile(D_MODEL), pl.BlockSpec((None, CONV_BUF, F2), lambda b, t: (b, 0, 0))],
        out_shape=[jax.ShapeDtypeStruct((n, length, D_MODEL), F32),
                   jax.ShapeDtypeStruct((n, CONV_BUF, F2), F32)],
        scratch_shapes=[khalf, khalf, vhalf, vhalf, pltpu.VMEM((rows, D_MODEL), BF16)]
        + _ffn_scratch(PROMPT_CONV_PRE, rows),
        compiler_params=_params(2),
        name="attn_layer_prompt",
    )(*args)


def _attn_post_layer(h, o, p, cpre, W, A, i, *, rows, step):
    steps, n_seq, _ = h.shape
    j = i - N_A
    pre_c = CONV_BUF * step
    tile, state = _row_specs(False, rows, step)
    kern = functools.partial(_attn_post_layer_kernel, rows=rows, step=step, pre_c=pre_c)
    return pl.pallas_call(
        kern,
        grid=(1, n_seq // step),
        in_specs=[tile(D_MODEL), tile(D_MODEL), tile(PLE_DIM, i), state(CONV_BUF, F2, i),
                  _resident((D_MODEL, D_MODEL), j)] + _ffn_weight_specs(i),
        out_specs=[tile(D_MODEL), state(CONV_BUF, F2)],
        out_shape=[jax.ShapeDtypeStruct(h.shape, F32),
                   jax.ShapeDtypeStruct((CONV_BUF, n_seq, F2), F32)],
        scratch_shapes=_ffn_scratch(pre_c, rows),
        compiler_params=_params(2),
        name="attn_post_layer_decode",
    )(h, o, p, cpre, A["w_o"], *_ffn_weights(W))


def _shared_kv(h2d, W, A, *, rows):
    total = h2d.shape[0]
    tile = lambda width: pl.BlockSpec((rows, width), lambda t: (t, 0))
    return pl.pallas_call(
        _kv_kernel,
        grid=(total // rows,),
        in_specs=[tile(D_MODEL), _resident((1, D_MODEL)), _resident((D_MODEL, KV_DIM)),
                  _resident((D_MODEL, KV_DIM)), _resident((KV_DIM, KV_DIM)), _resident((1, KV_DIM))],
        out_specs=[tile(KV_DIM), tile(KV_DIM)],
        out_shape=[jax.ShapeDtypeStruct((total, KV_DIM), F32)] * 2,
        compiler_params=_params(1),
        name="shared_kv",
    )(h2d, W["kv_norm"], A["w_k"], A["w_v"], A["bd_k"], A["knw"])


def _decode_attn(h_bm, knew, vnew, cache_k, cache_v, W, A, i):
    rows = h_bm.shape[0]
    n_seq = cache_k.shape[0]
    j = i - N_A
    kpad = A["bias_decode"].shape[1]
    qrows = N_KV * GROUP * PAIR_ROWS
    full = lambda r, width: pl.BlockSpec((r, width), lambda c: (0, 0))
    cache = pl.BlockSpec((SEQ_CHUNK, WINDOW, KV_DIM), lambda c: (c, 0, 0))
    kern = functools.partial(_decode_attn_kernel, kpad=kpad)
    ext = pltpu.VMEM((kpad, KV_DIM), F32)
    return pl.pallas_call(
        kern,
        grid=(n_seq // SEQ_CHUNK,),
        in_specs=[full(rows, D_MODEL), full(rows, KV_DIM), full(rows, KV_DIM), cache, cache,
                  _resident((1, D_MODEL), i), _resident((D_MODEL, D_MODEL), j),
                  _resident((D_MODEL, D_MODEL)), _resident((1, D_MODEL), j),
                  _resident((qrows, kpad)), _resident((qrows, 1), j),
                  _resident((qrows, KV_DIM)), _resident((qrows, 1))],
        out_specs=full(rows, D_MODEL),
        out_shape=jax.ShapeDtypeStruct((rows, D_MODEL), F32),
        scratch_shapes=[pltpu.VMEM((rows, D_MODEL), F32), ext, ext, ext, ext],
        compiler_params=_params(1),
        name="decode_attn",
    )(h_bm, knew, vnew, cache_k, cache_v, W["norm_mix"], A["w_q"], A["bd_q"], A["qnw"],
      A["bias_decode"], A["sink_decode"], A["kvmask"], A["row_a"])


def _t5_bucket(d):
    n = jnp.maximum(d, 0)
    max_exact = N_BUCKETS // 2
    nf = jnp.maximum(n, 1).astype(F32)
    large = max_exact + (jnp.log(nf / max_exact) / math.log(MAX_DISTANCE / max_exact)
                         * (N_BUCKETS - max_exact)).astype(jnp.int32)
    large = jnp.minimum(large, N_BUCKETS - 1)
    return jnp.where(n < max_exact, n, large)


def _block_diag_ones(width):
    idx = np.arange(width) // HEAD_DIM
    return jnp.asarray(idx[:, None] == idx[None, :], BF16)


def _banded_bias(tab):
    heads = tab.shape[1]
    period = 3 * BLOCK + 1
    base = jnp.concatenate([jnp.full((heads, BLOCK), NEG_INF, F32), tab[::-1].T,
                            jnp.full((heads, period - 2 * BLOCK), NEG_INF, F32)], axis=1)
    flat = jnp.tile(base, (1, BLOCK))[:, :BLOCK * (period - 1)]
    skew = flat.reshape(heads, BLOCK, period - 1)
    return skew[:, :, BLOCK - 1:3 * BLOCK - 1]


def _attention_tables(rel_bias, sinks):
    A = {}
    tab = rel_bias[_t5_bucket(jnp.arange(WINDOW))].astype(F32)

    per_head = _banded_bias(tab).reshape(N_KV // 2, 2, GROUP, BLOCK, 2 * BLOCK)
    steady = per_head.transpose(0, 1, 4, 2, 3).reshape(N_KV // 2, 4 * BLOCK, GROUP * BLOCK)
    first_mask = np.tile(np.arange(2 * BLOCK) >= BLOCK, 2)[None, :, None]
    A["bias_prompt"] = jnp.stack([jnp.where(first_mask, steady, NEG_INF), steady])

    tab3 = tab.reshape(WINDOW, N_KV, GROUP)
    steps = PAIR_ROWS // 2
    kpad = -(-(WINDOW + PAIR_ROWS) // 16) * 16
    col = np.arange(kpad)[None, :]
    r = np.arange(PAIR_ROWS)[:, None]
    step_of = r % steps
    seq_b = r >= steps
    keyidx = np.where(col < WINDOW, col, WINDOW + (col - WINDOW) % steps)
    own = (col < WINDOW) | ((col < WINDOW + PAIR_ROWS) & (((col - WINDOW) >= steps) == seq_b))
    dd = step_of + WINDOW - keyidx
    dvalid = own & (dd >= 0) & (dd < WINDOW)
    ddc = np.clip(dd, 0, WINDOW - 1)
    dec = jnp.where(dvalid[:, :, None, None], tab3[ddc], NEG_INF)
    A["bias_decode"] = dec.transpose(2, 3, 0, 1).reshape(N_KV * GROUP * PAIR_ROWS, kpad)
    A["row_a"] = jnp.asarray(np.tile(~seq_b, (N_KV * GROUP, 1)).reshape(-1, 1), F32)
    lane_kv = np.arange(KV_DIM)[None, :] // HEAD_DIM
    row_kv = np.arange(N_KV * GROUP * PAIR_ROWS)[:, None] // (GROUP * PAIR_ROWS)
    A["kvmask"] = jnp.asarray(lane_kv == row_kv, F32)

    sk = sinks.astype(F32).reshape(-1, N_KV, GROUP)
    A["sink_prompt"] = jnp.repeat(sk, BLOCK, axis=2)[:, :, None, :]
    A["sink_decode"] = jnp.repeat(sk.reshape(sk.shape[0], -1), PAIR_ROWS, axis=1)[..., None]
    return A


def _prepare(W):
    A = _attention_tables(W["rel_bias"], W["sinks"])
    n_b = W["w_q"].shape[0]
    A["w_q"] = (W["w_q"].reshape(n_b, D_MODEL, N_KV, GROUP, HEAD_DIM).transpose(0, 1, 3, 2, 4)
                .reshape(n_b, D_MODEL, D_MODEL).astype(BF16))
    A["w_o"] = (W["w_o"].reshape(n_b, N_KV, GROUP, HEAD_DIM, D_MODEL).transpose(0, 2, 1, 3, 4)
                .reshape(n_b, D_MODEL, D_MODEL).astype(BF16))
    A["qnw"] = jnp.tile(W["q_norm"], (1, N_HEADS)).reshape(n_b, 1, D_MODEL)
    A["knw"] = jnp.tile(W["k_norm"], N_KV).reshape(1, KV_DIM)
    A["w_k"] = W["w_k"].astype(BF16)
    A["w_v"] = W["w_v"].astype(BF16)
    A["bd_q"] = _block_diag_ones(D_MODEL)
    A["bd_k"] = _block_diag_ones(KV_DIM)
    V = dict(W)
    for name in ("w_pool", "w_up", "w_down", "w_ple_gate", "w_ple_proj"):
        V[name] = W[name].astype(BF16)
    for name in ("norm_mix", "norm_ffn", "norm_ple", "pool_scale", "conv_b"):
        V[name] = W[name][:, None, :]
    V["kv_norm"] = W["kv_norm"][None, :]
    return V, A


def kernel(x_prompt, x_sample, p_prompt, p_sample, state_pool, state_conv, cache_k, cache_v,
           norm_mix, norm_ffn, norm_ple, w_pool, pool_scale, kv_norm, w_k, w_v, k_norm,
           w_q, q_norm, sinks, w_o, rel_bias, w_up, conv_w, conv_b, w_down, w_ple_gate, w_ple_proj):
    W, A = _prepare(dict(
        norm_mix=norm_mix, norm_ffn=norm_ffn, norm_ple=norm_ple, w_pool=w_pool,
        pool_scale=pool_scale, kv_norm=kv_norm, w_k=w_k, w_v=w_v, k_norm=k_norm, w_q=w_q,
        q_norm=q_norm, sinks=sinks, w_o=w_o, rel_bias=rel_bias, w_up=w_up, conv_w=conv_w,
        conv_b=conv_b, w_down=w_down, w_ple_gate=w_ple_gate, w_ple_proj=w_ple_proj))
    batch, seq, _ = x_prompt.shape
    n_seq, steps, _ = x_sample.shape

    h = x_prompt
    pool_p, conv_p = [], []
    for i in range(N_A):
        h, ps, cs = _pool_layer(h, p_prompt, W, i, rows=PROMPT_TILE, step=1, prompt=True)
        pool_p.append(ps)
        conv_p.append(cs)
    k_p, v_p = _shared_kv(h.reshape(batch * seq, D_MODEL), W, A, rows=KV_TILE)
    k_p = k_p.reshape(batch, seq, KV_DIM)
    v_p = v_p.reshape(batch, seq, KV_DIM)
    for i in range(N_A, DEPTH):
        h, cs = _attn_prompt_layer(h, p_prompt, k_p, v_p, W, A, i, rows=PROMPT_TILE)
        conv_p.append(cs)
    y_prompt = h

    swap = lambda x: jnp.swapaxes(x, -3, -2)
    tile_rows = steps * DECODE_SEQ_TILE
    total = steps * n_seq
    hs = swap(x_sample)
    ps_sw, ppre_sw, cpre_sw = swap(p_sample), swap(state_pool), swap(state_conv)
    pool_s, conv_s = [], []
    for i in range(N_A):
        hs, ps, cs = _pool_layer(hs, ps_sw, W, i, rows=tile_rows, step=DECODE_SEQ_TILE,
                                 prompt=False, ppre=ppre_sw, cpre=cpre_sw)
        pool_s.append(ps)
        conv_s.append(cs)
    hs_bm = swap(hs).reshape(total, D_MODEL)
    k_new, v_new = _shared_kv(hs_bm, W, A, rows=total)
    ck = cache_k.reshape(n_seq, WINDOW, KV_DIM)
    cv = cache_v.reshape(n_seq, WINDOW, KV_DIM)
    for i in range(N_A, DEPTH):
        o_bm = _decode_attn(hs_bm, k_new, v_new, ck, cv, W, A, i)
        hs, cs = _attn_post_layer(hs, swap(o_bm.reshape(n_seq, steps, D_MODEL)), ps_sw, cpre_sw,
                                  W, A, i, rows=tile_rows, step=DECODE_SEQ_TILE)
        conv_s.append(cs)
        hs_bm = swap(hs).reshape(total, D_MODEL)
    y_sample = hs_bm.reshape(n_seq, steps, D_MODEL)

    def window(cache, new):
        ext = jnp.concatenate([cache, new.reshape(n_seq, steps, N_KV, HEAD_DIM)], axis=1)
        return ext[:, -WINDOW:]

    kp4 = k_p[:, -WINDOW:].reshape(batch, WINDOW, N_KV, HEAD_DIM)
    vp4 = v_p[:, -WINDOW:].reshape(batch, WINDOW, N_KV, HEAD_DIM)
    return (y_prompt, y_sample, jnp.stack(pool_p), swap(jnp.stack(pool_s)), jnp.stack(conv_p),
            swap(jnp.stack(conv_s)), kp4, window(cache_k, k_new), vp4, window(cache_v, v_new))
```

```python
import functools
import math

import numpy as np
import jax
import jax.numpy as jnp
from jax import lax
from jax.experimental import pallas as pl
from jax.experimental.pallas import tpu as pltpu

D_MODEL = 1024
DEPTH = 4
N_A = DEPTH // 2
POOL_WINDOWS = (2, 4, 8, 16)
POOL_GROUP = D_MODEL // len(POOL_WINDOWS)
POOL_BUF = max(POOL_WINDOWS) - 1
HEAD_DIM = 64
N_HEADS = D_MODEL // HEAD_DIM
N_KV = 4
GROUP = N_HEADS // N_KV
KV_DIM = N_KV * HEAD_DIM
WINDOW = 128
BLOCK = WINDOW
N_BUCKETS = 32
MAX_DISTANCE = 128
D_FF = 2816
F2 = 2 * D_FF
CONV_W = 3
CONV_BUF = CONV_W - 1
PLE_DIM = 256
EPS = 1e-6
PAST_LEN = 8192

LANES = 128
SUBLANES = 8
VMEM_LIMIT_BYTES = 58 * 1024 * 1024

PROMPT_TILE = 256
KV_TILE = 1024
FF_CHUNK = 512
UP_AHEAD = 2
DECODE_SEQ_TILE = 32
SEQ_CHUNK = 16
PAIR_ROWS = 2 * 4
PROMPT_POOL_PRE = 2 * SUBLANES
PROMPT_CONV_PRE = SUBLANES

F32 = jnp.float32
BF16 = jnp.bfloat16
NEG_INF = float("-inf")


def _rmsnorm(x, g):
    ms = jnp.mean(x * x, axis=-1, keepdims=True)
    return x * lax.rsqrt(ms + EPS) * g


def _gelu_tanh(x):
    c = np.float32(math.sqrt(2.0 / math.pi))
    cdf = 0.5 * (1.0 + jnp.tanh(c * (x + 0.044715 * (x * x * x))))
    return x * cdf


def _sigmoid(x):
    return 1.0 / (1.0 + jnp.exp(-x))


def _dot(a, b):
    return jnp.dot(a, b, preferred_element_type=F32)


def _dot_nt(a, b):
    return lax.dot_general(a, b, (((1,), (1,)), ((), ())), preferred_element_type=F32)


def _pool_stage(h, norm_g, carried, wpool_ref, scale, inv_cnt, pre, step, rows):
    xn = _rmsnorm(h, norm_g)
    ext = jnp.concatenate([carried, xn], axis=0)
    state = ext[pre + rows - POOL_BUF * step:pre + rows, :]
    carry = ext[rows:rows + pre, :]
    outs = []
    for g, w in enumerate(POOL_WINDOWS):
        cols = slice(g * POOL_GROUP, (g + 1) * POOL_GROUP)
        s = ext[:, cols]
        width = 1
        while width < w:
            s = s + pltpu.roll(s, width * step, 0)
            width *= 2
        d = s[pre:pre + rows, :] * inv_cnt[g] - xn[:, cols]
        outs.append(_dot(d.astype(BF16), wpool_ref[g]))
    return h + jnp.concatenate(outs, axis=1) * scale, state, carry


def _ffn_begin(h, norm_g, ucarry_ref, ubufs, wup_ref, cw_ref, cb_ref, wdown_ref, pre, step, rows):
    xn = _rmsnorm(h, norm_g).astype(BF16)
    bounds = list(range(0, D_FF, FF_CHUNK)) + [D_FF]
    n_chunks = len(bounds) - 1
    halves = lambda c: [slice(base + bounds[c], base + bounds[c + 1]) for base in (0, D_FF)]
    local = lambda c: [slice(j * FF_CHUNK, j * FF_CHUNK + bounds[c + 1] - bounds[c]) for j in range(2)]

    def up(c):
        ub = ubufs[c % len(ubufs)]
        for cols, loc in zip(halves(c), local(c)):
            ub[0:pre, loc] = ucarry_ref[:, cols]
            ub[pre:pre + rows, loc] = _dot(xn, wup_ref[:, cols])

    for c in range(min(UP_AHEAD, n_chunks)):
        up(c)

    def finish(live=None, between=()):
        f = None
        for c in range(n_chunks):
            if c + UP_AHEAD < n_chunks:
                up(c + UP_AHEAD)
            ub = ubufs[c % len(ubufs)]
            conv = []
            for cols, loc in zip(halves(c), local(c)):
                ue = ub[:, loc]
                tail = ue[rows:rows + pre, :]
                ucarry_ref[:, cols] = tail if live is None else jnp.where(live, tail, ue[0:pre, :])
                acc = cb_ref[:, cols] + pltpu.roll(ue, 2 * step, 0)[pre:pre + rows, :] * cw_ref[0:1, cols]
                acc = acc + pltpu.roll(ue, step, 0)[pre:pre + rows, :] * cw_ref[1:2, cols]
                conv.append(acc + ue[pre:pre + rows, :] * cw_ref[2:3, cols])
            a = (_gelu_tanh(conv[0]) * conv[1]).astype(BF16)
            part = _dot(a, wdown_ref[bounds[c]:bounds[c + 1], :])
            f = part if f is None else f + part
            if c < len(between):
                between[c]()
        return h + f

    return finish


def _ple(h, norm_g, p, wg_ref, wp_ref):
    xn = _rmsnorm(h, norm_g).astype(BF16)
    gate = _sigmoid(_dot(xn, wg_ref[...]))
    return h + gate * _dot(p.astype(BF16), wp_ref[...])


def _q_heads(h, norm_g, wq_ref, bd_ref, qnw):
    xn = _rmsnorm(h, norm_g).astype(BF16)
    q = _dot(xn, wq_ref[...])
    ssq = _dot((q * q).astype(BF16), bd_ref[...])
    return q * lax.rsqrt(ssq * (1.0 / HEAD_DIM) + EPS) * (qnw * HEAD_DIM ** -0.5)


def _banded_attention_pieces(h, first_tile, refs, store_result, rows):
    (kprev_ref, kcur_ref, vprev_ref, vcur_ref, nmix_ref, wq_ref, bd_ref, qnw_ref, bias_ref,
     sink_ref, wo_ref, klo_ref, khi_ref, vlo_ref, vhi_ref, o_ref) = refs
    low = lax.broadcasted_iota(jnp.int32, (1, LANES), 1) < HEAD_DIM
    top = lax.broadcasted_iota(jnp.int32, (LANES, 1), 0) < HEAD_DIM
    units = [(qb, pr) for qb in range(rows // BLOCK) for pr in range(N_KV // 2)]
    live = {}

    def scores(qb, pr):
        qrows = slice(qb * BLOCK, (qb + 1) * BLOCK)
        krows = slice(qb * BLOCK, (qb + 2) * BLOCK)
        qn = live["qn"]
        qs = jnp.concatenate(
            [qn[qrows, g * KV_DIM + pr * LANES:g * KV_DIM + (pr + 1) * LANES] for g in range(GROUP)],
            axis=0)
        kc = jnp.concatenate([klo_ref[pr, krows, :], khi_ref[pr, krows, :]], axis=0)
        return _dot_nt(kc, qs)

    def project():
        live["qn"] = _q_heads(h, nmix_ref[...], wq_ref, bd_ref, qnw_ref[...]).astype(BF16)
        kw = jnp.concatenate([kprev_ref[...], kcur_ref[...]], axis=0)
        vwt = jnp.concatenate([vprev_ref[...], vcur_ref[...]], axis=0).T
        ones_top = jnp.broadcast_to(jnp.where(top, 1.0, 0.0), (LANES, rows + BLOCK))
        for pr in range(N_KV // 2):
            lanes = slice(pr * LANES, (pr + 1) * LANES)
            klo_ref[pr] = jnp.where(low, kw[:, lanes], 0.0).astype(BF16)
            khi_ref[pr] = jnp.where(low, 0.0, kw[:, lanes]).astype(BF16)
            vlo_ref[pr] = jnp.concatenate([jnp.where(top, vwt[lanes, :], 0.0), ones_top],
                                          axis=0).astype(BF16)
            vhi_ref[pr] = jnp.concatenate([jnp.where(top, 0.0, vwt[lanes, :]), 1.0 - ones_top],
                                          axis=0).astype(BF16)
        live["s"] = scores(*units[0])

    def unit(n):
        qb, pr = units[n]
        s = live["s"]
        if n + 1 < len(units):
            live["s"] = scores(*units[n + 1])
        qrows = slice(qb * BLOCK, (qb + 1) * BLOCK)
        krows = slice(qb * BLOCK, (qb + 2) * BLOCK)
        table = jnp.where(first_tile, 0, 1) if qb == 0 else 1
        es, sinkterms = [], []
        for half in range(2):
            kv = 2 * pr + half
            bias = jnp.concatenate([bias_ref[table, kv * GROUP + g] for g in range(GROUP)], axis=1)
            sh = s[half * 2 * BLOCK:(half + 1) * 2 * BLOCK, :] + bias
            sink = sink_ref[kv]
            m = jnp.maximum(jnp.max(sh, axis=0, keepdims=True), sink)
            es.append(jnp.exp(sh - m).astype(BF16))
            sinkterms.append(jnp.exp(sink - m))
        vct = jnp.concatenate([vlo_ref[pr, :, krows], vhi_ref[pr, :, krows]], axis=1)
        ov = _dot(vct, jnp.concatenate(es, axis=0))
        denom = ov[LANES:, :] + jnp.where(top, sinkterms[0], sinkterms[1])
        o = (ov[:LANES, :] * (1.0 / denom)).T
        for g in range(GROUP):
            o_ref[qrows, g * KV_DIM + pr * LANES:g * KV_DIM + (pr + 1) * LANES] = (
                o[g * BLOCK:(g + 1) * BLOCK, :].astype(BF16))

    def output():
        store_result(h + _dot(o_ref[...], wo_ref[...]))

    return [project] + [functools.partial(unit, n) for n in range(len(units))] + [output]


def _pipeline_flags(n_tiles, total):
    s = pl.program_id(0)
    mixer_live = s < total
    ffn_live = s >= 1
    mixer_first = lax.rem(s, n_tiles) == 0
    ffn_first = lax.rem(s + n_tiles - 1, n_tiles) == 0
    return s, mixer_live, ffn_live, mixer_first, ffn_first


def _pool_prompt_kernel(h_ref, p_ref, nmix_ref, wpool_ref, pscale_ref,
                        nffn_ref, nple_ref, wup_ref, cw_ref, cb_ref, wdown_ref, wg_ref, wp_ref,
                        hout_ref, pstate_ref, cstate_ref,
                        pcarry_ref, ucarry_ref, *ubufs, rows):
    t = pl.program_id(1)
    pre_p, pre_c = PROMPT_POOL_PRE, PROMPT_CONV_PRE

    @pl.when(t == 0)
    def _():
        pcarry_ref[...] = jnp.zeros((pre_p, D_MODEL), F32)
        ucarry_ref[...] = jnp.zeros((pre_c, F2), F32)

    pos = t * rows + lax.broadcasted_iota(jnp.int32, (rows, POOL_GROUP), 0)
    inv_cnt = [1.0 / jnp.minimum(pos + 1, w).astype(F32) for w in POOL_WINDOWS]
    h, state, carry = _pool_stage(h_ref[...], nmix_ref[...], pcarry_ref[...], wpool_ref,
                                  pscale_ref[...], inv_cnt, pre_p, 1, rows)
    pstate_ref[...] = state
    pcarry_ref[...] = carry
    h = _ffn_begin(h, nffn_ref[...], ucarry_ref, list(ubufs), wup_ref, cw_ref, cb_ref, wdown_ref,
                   pre_c, 1, rows)()
    cstate_ref[...] = ucarry_ref[pre_c - CONV_BUF:pre_c, :]
    hout_ref[...] = _ple(h, nple_ref[...], p_ref[...], wg_ref, wp_ref)


def _attn_prompt_kernel(h_ref, p_ref, kprev_ref, kcur_ref, vprev_ref, vcur_ref,
                        nmix_ref, wq_ref, bd_ref, qnw_ref, bias_ref, sink_ref, wo_ref,
                        nffn_ref, nple_ref, wup_ref, cw_ref, cb_ref, wdown_ref, wg_ref, wp_ref,
                        hout_ref, cstate_ref,
                        hmid_ref, klo_ref, khi_ref, vlo_ref, vhi_ref, o_ref, ucarry_ref, *ubufs,
                        rows, n_tiles, total):
    s, _, ffn_live, mixer_first, ffn_first = _pipeline_flags(n_tiles, total)
    pre_c = PROMPT_CONV_PRE

    @pl.when(s == 0)
    def _():
        hmid_ref[...] = jnp.zeros(hmid_ref.shape, F32)

    @pl.when(jnp.logical_or(ffn_first, s == 0))
    def _():
        ucarry_ref[...] = jnp.zeros((pre_c, F2), F32)

    slot = lax.rem(s, 2)
    finish = _ffn_begin(hmid_ref[1 - slot], nffn_ref[...], ucarry_ref, list(ubufs), wup_ref, cw_ref,
                        cb_ref, wdown_ref, pre_c, 1, rows)

    def store_result(value):
        hmid_ref[slot] = value

    pieces = _banded_attention_pieces(
        h_ref[...], mixer_first,
        (kprev_ref, kcur_ref, vprev_ref, vcur_ref, nmix_ref, wq_ref, bd_ref, qnw_ref, bias_ref,
         sink_ref, wo_ref, klo_ref, khi_ref, vlo_ref, vhi_ref, o_ref), store_result, rows)
    n_between = -(-D_FF // FF_CHUNK) - 1
    lead = max(len(pieces) - n_between, 1)
    for piece in pieces[:lead]:
        piece()
    h = finish(ffn_live, pieces[lead:])
    cstate_ref[...] = ucarry_ref[pre_c - CONV_BUF:pre_c, :]
    hout_ref[...] = _ple(h, nple_ref[...], p_ref[...], wg_ref, wp_ref)


def _decode_ffn_ple(h, refs, pre_c, step, rows):
    (p_ref, cpre_ref, nffn_ref, nple_ref, wup_ref, cw_ref, cb_ref, wdown_ref, wg_ref, wp_ref,
     hout_ref, cstate_ref, ucarry_ref, ubufs) = refs
    ucarry_ref[...] = cpre_ref[...].reshape(pre_c, F2)
    h = _ffn_begin(h, nffn_ref[...], ucarry_ref, ubufs, wup_ref, cw_ref, cb_ref, wdown_ref,
                   pre_c, step, rows)()
    cstate_ref[...] = ucarry_ref[...].reshape(cstate_ref.shape)
    p = p_ref[...].reshape(rows, PLE_DIM)
    hout_ref[...] = _ple(h, nple_ref[...], p, wg_ref, wp_ref).reshape(hout_ref.shape)


def _pool_decode_kernel(h_ref, p_ref, ppre_ref, cpre_ref, nmix_ref, wpool_ref, pscale_ref,
                        nffn_ref, nple_ref, wup_ref, cw_ref, cb_ref, wdown_ref, wg_ref, wp_ref,
                        hout_ref, pstate_ref, cstate_ref, ucarry_ref, *ubufs, rows, step):
    pre_p, pre_c = POOL_BUF * step, CONV_BUF * step
    inv_cnt = [1.0 / min(PAST_LEN + 1, w) for w in POOL_WINDOWS]
    h, state, _ = _pool_stage(h_ref[...].reshape(rows, D_MODEL), nmix_ref[...],
                              ppre_ref[...].reshape(pre_p, D_MODEL), wpool_ref, pscale_ref[...],
                              inv_cnt, pre_p, step, rows)
    pstate_ref[...] = state.reshape(pstate_ref.shape)
    _decode_ffn_ple(h, (p_ref, cpre_ref, nffn_ref, nple_ref, wup_ref, cw_ref, cb_ref, wdown_ref,
                        wg_ref, wp_ref, hout_ref, cstate_ref, ucarry_ref, list(ubufs)),
                    pre_c, step, rows)


def _attn_post_decode_kernel(h_ref, o_ref, p_ref, cpre_ref, wo_ref,
                             nffn_ref, nple_ref, wup_ref, cw_ref, cb_ref, wdown_ref, wg_ref, wp_ref,
                             hout_ref, cstate_ref, ucarry_ref, *ubufs, rows, step):
    h = h_ref[...].reshape(rows, D_MODEL) + _dot(o_ref[...].reshape(rows, D_MODEL).astype(BF16),
                                                 wo_ref[...])
    _decode_ffn_ple(h, (p_ref, cpre_ref, nffn_ref, nple_ref, wup_ref, cw_ref, cb_ref, wdown_ref,
                        wg_ref, wp_ref, hout_ref, cstate_ref, ucarry_ref, list(ubufs)),
                    CONV_BUF * step, step, rows)


def _kv_kernel(h_ref, norm_ref, wk_ref, wv_ref, bd_ref, knw_ref, k_ref, v_ref):
    xn = _rmsnorm(h_ref[...], norm_ref[...]).astype(BF16)
    k = _dot(xn, wk_ref[...])
    ssq = _dot((k * k).astype(BF16), bd_ref[...])
    k_ref[...] = k * lax.rsqrt(ssq * (1.0 / HEAD_DIM) + EPS) * knw_ref[...]
    v_ref[...] = _dot(xn, wv_ref[...])


def _softmax_sink(s, sink):
    m = jnp.maximum(jnp.max(s, axis=-1, keepdims=True), sink)
    e = jnp.exp(s - m)
    denom = jnp.sum(e, axis=-1, keepdims=True) + jnp.exp(sink - m)
    return e * (1.0 / denom)


def _decode_attn_kernel(h_ref, knew_ref, vnew_ref, ck_ref, cv_ref,
                        nmix_ref, wq_ref, bd_ref, qnw_ref, bias_ref, sink_ref, kvmask_ref, rowa_ref,
                        o_ref, qn_ref, kxa_ref, kxb_ref, vxa_ref, vxb_ref, *, kpad):
    c = pl.program_id(0)

    @pl.when(c == 0)
    def _():
        qn_ref[...] = _q_heads(h_ref[...], nmix_ref[...], wq_ref, bd_ref, qnw_ref[...])
        zeros = jnp.zeros((kpad - WINDOW - PAIR_ROWS, KV_DIM), F32)
        for ref in (kxa_ref, kxb_ref, vxa_ref, vxb_ref):
            ref[WINDOW + PAIR_ROWS:kpad, :] = zeros

    kvmask = kvmask_ref[...]
    row_a = rowa_ref[...]
    row_b = 1.0 - row_a
    bias = bias_ref[...]
    sink = sink_ref[...]

    def pair(i, carry):
        r0 = pl.multiple_of((c * (SEQ_CHUNK // 2) + i) * PAIR_ROWS, PAIR_ROWS)
        q8 = qn_ref[pl.ds(r0, PAIR_ROWS), :]
        qg = jnp.concatenate([q8[:, g * KV_DIM:(g + 1) * KV_DIM] for g in range(GROUP)], axis=0)
        qb = (jnp.concatenate([qg] * N_KV, axis=0) * kvmask).astype(BF16)
        knew = knew_ref[pl.ds(r0, PAIR_ROWS), :]
        vnew = vnew_ref[pl.ds(r0, PAIR_ROWS), :]
        for ref, cache, new, j in ((kxa_ref, ck_ref, knew, 0), (kxb_ref, ck_ref, knew, 1),
                                   (vxa_ref, cv_ref, vnew, 0), (vxb_ref, cv_ref, vnew, 1)):
            ref[0:WINDOW, :] = cache[2 * i + j]
            ref[WINDOW:WINDOW + PAIR_ROWS, :] = new
        sa = _dot_nt(qb, kxa_ref[...].astype(BF16))
        sb = _dot_nt(qb, kxb_ref[...].astype(BF16))
        pr = _softmax_sink(sa * row_a + sb * row_b + bias, sink)
        pa = (pr * row_a).astype(BF16)
        pb = (pr * row_b).astype(BF16)
        o = (_dot(pa, vxa_ref[...].astype(BF16)) + _dot(pb, vxb_ref[...].astype(BF16))) * kvmask
        blk = GROUP * PAIR_ROWS
        og = o[0:blk] + o[blk:2 * blk] + o[2 * blk:3 * blk] + o[3 * blk:4 * blk]
        for g in range(GROUP):
            o_ref[pl.ds(r0, PAIR_ROWS), g * KV_DIM:(g + 1) * KV_DIM] = og[g * PAIR_ROWS:(g + 1) * PAIR_ROWS]
        return carry

    lax.fori_loop(0, SEQ_CHUNK // 2, pair, 0)


def _resident(shape, layer=None):
    zeros = (0,) * len(shape)
    if layer is None:
        return pl.BlockSpec(shape, lambda *_: zeros, pipeline_mode=pl.Buffered(1))
    return pl.BlockSpec((None,) + shape, lambda *_: (layer,) + zeros, pipeline_mode=pl.Buffered(1))


def _params(n_axes):
    return pltpu.CompilerParams(dimension_semantics=("arbitrary",) * n_axes,
                                vmem_limit_bytes=VMEM_LIMIT_BYTES)


def _ffn_weight_specs(i):
    return [_resident((1, D_MODEL), i), _resident((1, D_MODEL), i), _resident((D_MODEL, F2), i),
            _resident((CONV_W, F2), i), _resident((1, F2), i), _resident((D_FF, D_MODEL), i),
            _resident((D_MODEL, D_MODEL), i), _resident((PLE_DIM, D_MODEL), i)]


def _ffn_weights(W):
    return [W["norm_ffn"], W["norm_ple"], W["w_up"], W["conv_w"], W["conv_b"], W["w_down"],
            W["w_ple_gate"], W["w_ple_proj"]]


def _ffn_scratch(pre_c, rows):
    return ([pltpu.VMEM((pre_c, F2), F32)]
            + [pltpu.VMEM((pre_c + rows, 2 * FF_CHUNK), F32) for _ in range(UP_AHEAD + 1)])


def _prompt_specs(rows, n_tiles, total):
    mixer = lambda s: jnp.minimum(s, total - 1)
    ffn = lambda s: jnp.maximum(s - 1, 0)

    def make(which, per_sequence):
        def build(r, width, layer=None):
            lead = () if layer is None else (layer,)
            index = lambda s: lead + (which(s) // n_tiles, 0 if per_sequence else which(s) % n_tiles, 0)
            return pl.BlockSpec((None,) * (len(lead) + 1) + (r, width), index)
        return build

    return make(mixer, False), make(ffn, False), make(mixer, True), make(ffn, True)


def _pool_prompt_layer(h, p, W, i, *, rows):
    batch, length, _ = h.shape
    assert length % rows == 0
    tile = lambda width: pl.BlockSpec((None, rows, width), lambda b, t: (b, t, 0))
    state = lambda r, width: pl.BlockSpec((None, r, width), lambda b, t: (b, 0, 0))
    in_specs = [tile(D_MODEL), pl.BlockSpec((None, None, rows, PLE_DIM), lambda b, t: (i, b, t, 0)),
                _resident((1, D_MODEL), i),
                _resident((len(POOL_WINDOWS), POOL_GROUP, POOL_GROUP), i),
                _resident((1, D_MODEL), i)] + _ffn_weight_specs(i)
    kern = functools.partial(_pool_prompt_kernel, rows=rows)
    return pl.pallas_call(
        kern,
        grid=(batch, length // rows),
        in_specs=in_specs,
        out_specs=[tile(D_MODEL), state(POOL_BUF, D_MODEL), state(CONV_BUF, F2)],
        out_shape=[jax.ShapeDtypeStruct(h.shape, F32),
                   jax.ShapeDtypeStruct((batch, POOL_BUF, D_MODEL), F32),
                   jax.ShapeDtypeStruct((batch, CONV_BUF, F2), F32)],
        scratch_shapes=[pltpu.VMEM((PROMPT_POOL_PRE, D_MODEL), F32)]
        + _ffn_scratch(PROMPT_CONV_PRE, rows),
        compiler_params=_params(2),
        name="pool_layer_prompt",
    )(h, p, W["norm_mix"], W["w_pool"], W["pool_scale"], *_ffn_weights(W))


def _attn_prompt_layer(h, p, k, v, W, A, i, *, rows):
    batch, length, _ = h.shape
    j = i - N_A
    n_tiles = length // rows
    total = batch * n_tiles
    blocks = rows // BLOCK
    assert n_tiles > 1 and length % rows == 0
    mixer_tile, ffn_tile, _, ffn_state = _prompt_specs(rows, n_tiles, total)
    mixer = lambda s: jnp.minimum(s, total - 1)
    prev = pl.BlockSpec(
        (None, BLOCK, KV_DIM),
        lambda s: (mixer(s) // n_tiles, jnp.maximum((mixer(s) % n_tiles) * blocks - 1, 0), 0))
    in_specs = [mixer_tile(rows, D_MODEL), ffn_tile(rows, PLE_DIM, i),
                prev, mixer_tile(rows, KV_DIM), prev, mixer_tile(rows, KV_DIM),
                _resident((1, D_MODEL), i), _resident((D_MODEL, D_MODEL), j),
                _resident((D_MODEL, D_MODEL)), _resident((1, D_MODEL), j),
                _resident((2, N_HEADS, 2 * BLOCK, BLOCK)),
                _resident((N_KV, 1, GROUP * BLOCK), j), _resident((D_MODEL, D_MODEL), j)]
    in_specs += _ffn_weight_specs(i)
    args = [h, p, k, k, v, v, W["norm_mix"], A["w_q"], A["bd_q"], A["qnw"],
            A["bias_prompt"], A["sink_prompt"], A["w_o"]] + _ffn_weights(W)
    kern = functools.partial(_attn_prompt_kernel, rows=rows, n_tiles=n_tiles, total=total)
    khalf = pltpu.VMEM((N_KV // 2, rows + BLOCK, LANES), BF16)
    vhalf = pltpu.VMEM((N_KV // 2, 2 * LANES, rows + BLOCK), BF16)
    return pl.pallas_call(
        kern,
        grid=(total + 1,),
        in_specs=in_specs,
        out_specs=[ffn_tile(rows, D_MODEL), ffn_state(CONV_BUF, F2)],
        out_shape=[jax.ShapeDtypeStruct(h.shape, F32),
                   jax.ShapeDtypeStruct((batch, CONV_BUF, F2), F32)],
        scratch_shapes=[pltpu.VMEM((2, rows, D_MODEL), F32), khalf, khalf, vhalf, vhalf,
                        pltpu.VMEM((rows, D_MODEL), BF16)] + _ffn_scratch(PROMPT_CONV_PRE, rows),
        compiler_params=_params(1),
        name="attn_layer_prompt",
    )(*args)


def _decode_specs(rows, step):
    def build(r, width, layer=None):
        lead = () if layer is None else (layer,)
        return pl.BlockSpec((None,) * len(lead) + (r, step, width), lambda t: lead + (0, t, 0))
    return build


def _pool_decode_layer(h, p, ppre, cpre, W, i, *, rows, step):
    steps, n_seq, _ = h.shape
    block = _decode_specs(rows, step)
    in_specs = [block(steps, D_MODEL), block(steps, PLE_DIM, i), block(POOL_BUF, D_MODEL, i),
                block(CONV_BUF, F2, i), _resident((1, D_MODEL), i),
                _resident((len(POOL_WINDOWS), POOL_GROUP, POOL_GROUP), i),
                _resident((1, D_MODEL), i)] + _ffn_weight_specs(i)
    kern = functools.partial(_pool_decode_kernel, rows=rows, step=step)
    return pl.pallas_call(
        kern,
        grid=(n_seq // step,),
        in_specs=in_specs,
        out_specs=[block(steps, D_MODEL), block(POOL_BUF, D_MODEL), block(CONV_BUF, F2)],
        out_shape=[jax.ShapeDtypeStruct(h.shape, F32),
                   jax.ShapeDtypeStruct((POOL_BUF, n_seq, D_MODEL), F32),
                   jax.ShapeDtypeStruct((CONV_BUF, n_seq, F2), F32)],
        scratch_shapes=_ffn_scratch(CONV_BUF * step, rows),
        compiler_params=_params(1),
        name="pool_layer_decode",
    )(h, p, ppre, cpre, W["norm_mix"], W["w_pool"], W["pool_scale"], *_ffn_weights(W))


def _attn_post_decode_layer(h, o, p, cpre, W, A, i, *, rows, step):
    steps, n_seq, _ = h.shape
    j = i - N_A
    block = _decode_specs(rows, step)
    kern = functools.partial(_attn_post_decode_kernel, rows=rows, step=step)
    return pl.pallas_call(
        kern,
        grid=(n_seq // step,),
        in_specs=[block(steps, D_MODEL), block(steps, D_MODEL), block(steps, PLE_DIM, i),
                  block(CONV_BUF, F2, i), _resident((D_MODEL, D_MODEL), j)] + _ffn_weight_specs(i),
        out_specs=[block(steps, D_MODEL), block(CONV_BUF, F2)],
        out_shape=[jax.ShapeDtypeStruct(h.shape, F32),
                   jax.ShapeDtypeStruct((CONV_BUF, n_seq, F2), F32)],
        scratch_shapes=_ffn_scratch(CONV_BUF * step, rows),
        compiler_params=_params(1),
        name="attn_post_layer_decode",
    )(h, o, p, cpre, A["w_o"], *_ffn_weights(W))


def _shared_kv(h2d, W, A, *, rows):
    total = h2d.shape[0]
    tile = lambda width: pl.BlockSpec((rows, width), lambda t: (t, 0))
    return pl.pallas_call(
        _kv_kernel,
        grid=(total // rows,),
        in_specs=[tile(D_MODEL), _resident((1, D_MODEL)), _resident((D_MODEL, KV_DIM)),
                  _resident((D_MODEL, KV_DIM)), _resident((KV_DIM, KV_DIM)), _resident((1, KV_DIM))],
        out_specs=[tile(KV_DIM), tile(KV_DIM)],
        out_shape=[jax.ShapeDtypeStruct((total, KV_DIM), F32)] * 2,
        compiler_params=_params(1),
        name="shared_kv",
    )(h2d, W["kv_norm"], A["w_k"], A["w_v"], A["bd_k"], A["knw"])


def _decode_attn(h_bm, knew, vnew, cache_k, cache_v, W, A, i):
    rows = h_bm.shape[0]
    n_seq = cache_k.shape[0]
    j = i - N_A
    kpad = A["bias_decode"].shape[1]
    qrows = N_KV * GROUP * PAIR_ROWS
    full = lambda r, width: pl.BlockSpec((r, width), lambda c: (0, 0))
    cache = pl.BlockSpec((SEQ_CHUNK, WINDOW, KV_DIM), lambda c: (c, 0, 0))
    kern = functools.partial(_decode_attn_kernel, kpad=kpad)
    ext = pltpu.VMEM((kpad, KV_DIM), F32)
    return pl.pallas_call(
        kern,
        grid=(n_seq // SEQ_CHUNK,),
        in_specs=[full(rows, D_MODEL), full(rows, KV_DIM), full(rows, KV_DIM), cache, cache,
                  _resident((1, D_MODEL), i), _resident((D_MODEL, D_MODEL), j),
                  _resident((D_MODEL, D_MODEL)), _resident((1, D_MODEL), j),
                  _resident((qrows, kpad)), _resident((qrows, 1), j),
                  _resident((qrows, KV_DIM)), _resident((qrows, 1))],
        out_specs=full(rows, D_MODEL),
        out_shape=jax.ShapeDtypeStruct((rows, D_MODEL), F32),
        scratch_shapes=[pltpu.VMEM((rows, D_MODEL), F32), ext, ext, ext, ext],
        compiler_params=_params(1),
        name="decode_attn",
    )(h_bm, knew, vnew, cache_k, cache_v, W["norm_mix"], A["w_q"], A["bd_q"], A["qnw"],
      A["bias_decode"], A["sink_decode"], A["kvmask"], A["row_a"])


def _t5_bucket(d):
    n = jnp.maximum(d, 0)
    max_exact = N_BUCKETS // 2
    nf = jnp.maximum(n, 1).astype(F32)
    large = max_exact + (jnp.log(nf / max_exact) / math.log(MAX_DISTANCE / max_exact)
                         * (N_BUCKETS - max_exact)).astype(jnp.int32)
    large = jnp.minimum(large, N_BUCKETS - 1)
    return jnp.where(n < max_exact, n, large)


def _block_diag_ones(width):
    idx = np.arange(width) // HEAD_DIM
    return jnp.asarray(idx[:, None] == idx[None, :], BF16)


def _banded_bias(tab):
    heads = tab.shape[1]
    period = 3 * BLOCK + 1
    base = jnp.concatenate([jnp.full((heads, BLOCK - 1), NEG_INF, F32), tab.T,
                            jnp.full((heads, period - 2 * BLOCK + 1), NEG_INF, F32)], axis=1)
    flat = jnp.tile(base, (1, 2 * BLOCK))[:, :2 * BLOCK * (period - 1)]
    skew = flat.reshape(heads, 2 * BLOCK, period - 1)
    return skew[:, :, 2 * BLOCK - 1:3 * BLOCK - 1]


def _attention_tables(rel_bias, sinks):
    A = {}
    tab = rel_bias[_t5_bucket(jnp.arange(WINDOW))].astype(F32)

    steady = _banded_bias(tab)
    has_prev = (np.arange(2 * BLOCK) >= BLOCK)[None, :, None]
    A["bias_prompt"] = jnp.stack([jnp.where(has_prev, steady, NEG_INF), steady])

    tab3 = tab.reshape(WINDOW, N_KV, GROUP)
    steps = PAIR_ROWS // 2
    kpad = -(-(WINDOW + PAIR_ROWS) // 16) * 16
    col = np.arange(kpad)[None, :]
    r = np.arange(PAIR_ROWS)[:, None]
    step_of = r % steps
    seq_b = r >= steps
    keyidx = np.where(col < WINDOW, col, WINDOW + (col - WINDOW) % steps)
    own = (col < WINDOW) | ((col < WINDOW + PAIR_ROWS) & (((col - WINDOW) >= steps) == seq_b))
    dd = step_of + WINDOW - keyidx
    dvalid = own & (dd >= 0) & (dd < WINDOW)
    ddc = np.clip(dd, 0, WINDOW - 1)
    dec = jnp.where(dvalid[:, :, None, None], tab3[ddc], NEG_INF)
    A["bias_decode"] = dec.transpose(2, 3, 0, 1).reshape(N_KV * GROUP * PAIR_ROWS, kpad)
    A["row_a"] = jnp.asarray(np.tile(~seq_b, (N_KV * GROUP, 1)).reshape(-1, 1), F32)
    lane_kv = np.arange(KV_DIM)[None, :] // HEAD_DIM
    row_kv = np.arange(N_KV * GROUP * PAIR_ROWS)[:, None] // (GROUP * PAIR_ROWS)
    A["kvmask"] = jnp.asarray(lane_kv == row_kv, F32)

    sk = sinks.astype(F32).reshape(-1, N_KV, GROUP)
    A["sink_prompt"] = jnp.repeat(sk, BLOCK, axis=2)[:, :, None, :]
    A["sink_decode"] = jnp.repeat(sk.reshape(sk.shape[0], -1), PAIR_ROWS, axis=1)[..., None]
    return A


def _prepare(W):
    A = _attention_tables(W["rel_bias"], W["sinks"])
    n_b = W["w_q"].shape[0]
    A["w_q"] = (W["w_q"].reshape(n_b, D_MODEL, N_KV, GROUP, HEAD_DIM).transpose(0, 1, 3, 2, 4)
                .reshape(n_b, D_MODEL, D_MODEL).astype(BF16))
    A["w_o"] = (W["w_o"].reshape(n_b, N_KV, GROUP, HEAD_DIM, D_MODEL).transpose(0, 2, 1, 3, 4)
                .reshape(n_b, D_MODEL, D_MODEL).astype(BF16))
    A["qnw"] = jnp.tile(W["q_norm"], (1, N_HEADS)).reshape(n_b, 1, D_MODEL)
    A["knw"] = jnp.tile(W["k_norm"], N_KV).reshape(1, KV_DIM)
    A["w_k"] = W["w_k"].astype(BF16)
    A["w_v"] = W["w_v"].astype(BF16)
    A["bd_q"] = _block_diag_ones(D_MODEL)
    A["bd_k"] = _block_diag_ones(KV_DIM)
    V = dict(W)
    for name in ("w_pool", "w_up", "w_down", "w_ple_gate", "w_ple_proj"):
        V[name] = W[name].astype(BF16)
    for name in ("norm_mix", "norm_ffn", "norm_ple", "pool_scale", "conv_b"):
        V[name] = W[name][:, None, :]
    V["kv_norm"] = W["kv_norm"][None, :]
    return V, A


def kernel(x_prompt, x_sample, p_prompt, p_sample, state_pool, state_conv, cache_k, cache_v,
           norm_mix, norm_ffn, norm_ple, w_pool, pool_scale, kv_norm, w_k, w_v, k_norm,
           w_q, q_norm, sinks, w_o, rel_bias, w_up, conv_w, conv_b, w_down, w_ple_gate, w_ple_proj):
    W, A = _prepare(dict(
        norm_mix=norm_mix, norm_ffn=norm_ffn, norm_ple=norm_ple, w_pool=w_pool,
        pool_scale=pool_scale, kv_norm=kv_norm, w_k=w_k, w_v=w_v, k_norm=k_norm, w_q=w_q,
        q_norm=q_norm, sinks=sinks, w_o=w_o, rel_bias=rel_bias, w_up=w_up, conv_w=conv_w,
        conv_b=conv_b, w_down=w_down, w_ple_gate=w_ple_gate, w_ple_proj=w_ple_proj))
    batch, seq, _ = x_prompt.shape
    n_seq, steps, _ = x_sample.shape

    h = x_prompt
    pool_p, conv_p = [], []
    for i in range(N_A):
        h, ps, cs = _pool_prompt_layer(h, p_prompt, W, i, rows=PROMPT_TILE)
        pool_p.append(ps)
        conv_p.append(cs)
    k_p, v_p = _shared_kv(h.reshape(batch * seq, D_MODEL), W, A, rows=KV_TILE)
    k_p = k_p.reshape(batch, seq, KV_DIM)
    v_p = v_p.reshape(batch, seq, KV_DIM)
    for i in range(N_A, DEPTH):
        h, cs = _attn_prompt_layer(h, p_prompt, k_p, v_p, W, A, i, rows=PROMPT_TILE)
        conv_p.append(cs)
    y_prompt = h

    swap = lambda x: jnp.swapaxes(x, -3, -2)
    tile_rows = steps * DECODE_SEQ_TILE
    total = steps * n_seq
    hs = swap(x_sample)
    ps_sw, ppre_sw, cpre_sw = swap(p_sample), swap(state_pool), swap(state_conv)
    pool_s, conv_s = [], []
    for i in range(N_A):
        hs, ps, cs = _pool_decode_layer(hs, ps_sw, ppre_sw, cpre_sw, W, i, rows=tile_rows,
                                        step=DECODE_SEQ_TILE)
        pool_s.append(ps)
        conv_s.append(cs)
    hs_bm = swap(hs).reshape(total, D_MODEL)
    k_new, v_new = _shared_kv(hs_bm, W, A, rows=total)
    ck = cache_k.reshape(n_seq, WINDOW, KV_DIM)
    cv = cache_v.reshape(n_seq, WINDOW, KV_DIM)
    for i in range(N_A, DEPTH):
        o_bm = _decode_attn(hs_bm, k_new, v_new, ck, cv, W, A, i)
        hs, cs = _attn_post_decode_layer(hs, swap(o_bm.reshape(n_seq, steps, D_MODEL)), ps_sw,
                                         cpre_sw, W, A, i, rows=tile_rows, step=DECODE_SEQ_TILE)
        conv_s.append(cs)
        hs_bm = swap(hs).reshape(total, D_MODEL)
    y_sample = hs_bm.reshape(n_seq, steps, D_MODEL)

    def window(cache, new):
        ext = jnp.concatenate([cache, new.reshape(n_seq, steps, N_KV, HEAD_DIM)], axis=1)
        return ext[:, -WINDOW:]

    kp4 = k_p[:, -WINDOW:].reshape(batch, WINDOW, N_KV, HEAD_DIM)
    vp4 = v_p[:, -WINDOW:].reshape(batch, WINDOW, N_KV, HEAD_DIM)
    return (y_prompt, y_sample, jnp.stack(pool_p), swap(jnp.stack(pool_s)), jnp.stack(conv_p),
            swap(jnp.stack(conv_s)), kp4, window(cache_k, k_new), vp4, window(cache_v, v_new))
```

```python
import functools
import math

import numpy as np
import jax
import jax.numpy as jnp
from jax import lax
from jax.experimental import pallas as pl
from jax.experimental.pallas import tpu as pltpu

D_MODEL = 1024
DEPTH = 4
N_A = DEPTH // 2
POOL_WINDOWS = (2, 4, 8, 16)
POOL_GROUP = D_MODEL // len(POOL_WINDOWS)
POOL_BUF = max(POOL_WINDOWS) - 1
HEAD_DIM = 64
N_HEADS = D_MODEL // HEAD_DIM
N_KV = 4
GROUP = N_HEADS // N_KV
KV_DIM = N_KV * HEAD_DIM
WINDOW = 128
BLOCK = WINDOW
N_BUCKETS = 32
MAX_DISTANCE = 128
D_FF = 2816
F2 = 2 * D_FF
CONV_W = 3
CONV_BUF = CONV_W - 1
PLE_DIM = 256
EPS = 1e-6
PAST_LEN = 8192

LANES = 128
SUBLANES = 8
VMEM_LIMIT_BYTES = 58 * 1024 * 1024

PROMPT_TILE = 256
KV_TILE = 1024
FF_CHUNK = 512
UP_AHEAD = 2
DECODE_SEQ_TILE = 64
SEQ_CHUNK = 16
PAIR_ROWS = 2 * 4
PROMPT_POOL_PRE = 2 * SUBLANES
PROMPT_CONV_PRE = SUBLANES

F32 = jnp.float32
BF16 = jnp.bfloat16
NEG_INF = float("-inf")


def _rmsnorm(x, g):
    ms = jnp.mean(x * x, axis=-1, keepdims=True)
    return x * lax.rsqrt(ms + EPS) * g


def _gelu_tanh(x):
    c = np.float32(math.sqrt(2.0 / math.pi))
    cdf = 0.5 * (1.0 + jnp.tanh(c * (x + 0.044715 * (x * x * x))))
    return x * cdf


def _sigmoid(x):
    return 1.0 / (1.0 + jnp.exp(-x))


def _dot(a, b):
    return jnp.dot(a, b, preferred_element_type=F32)


def _dot_nt(a, b):
    return lax.dot_general(a, b, (((1,), (1,)), ((), ())), preferred_element_type=F32)


def _pool_stage(h, norm_g, carried, wpool_ref, scale, inv_cnt, pre, step, rows):
    xn = _rmsnorm(h, norm_g)
    ext = jnp.concatenate([carried, xn], axis=0)
    state = ext[pre + rows - POOL_BUF * step:pre + rows, :]
    carry = ext[rows:rows + pre, :]
    outs = []
    for g, w in enumerate(POOL_WINDOWS):
        cols = slice(g * POOL_GROUP, (g + 1) * POOL_GROUP)
        s = ext[:, cols]
        width = 1
        while width < w:
            s = s + pltpu.roll(s, width * step, 0)
            width *= 2
        d = s[pre:pre + rows, :] * inv_cnt[g] - xn[:, cols]
        outs.append(_dot(d.astype(BF16), wpool_ref[g]))
    return h + jnp.concatenate(outs, axis=1) * scale, state, carry


def _ffn_begin(h, norm_g, ucarry_ref, ubufs, wup_ref, cw_ref, cb_ref, wdown_ref, pre, step, rows):
    xn = _rmsnorm(h, norm_g).astype(BF16)
    bounds = list(range(0, D_FF, FF_CHUNK)) + [D_FF]
    n_chunks = len(bounds) - 1
    halves = lambda c: [slice(base + bounds[c], base + bounds[c + 1]) for base in (0, D_FF)]
    local = lambda c: [slice(j * FF_CHUNK, j * FF_CHUNK + bounds[c + 1] - bounds[c]) for j in range(2)]

    def up(c):
        ub = ubufs[c % len(ubufs)]
        for cols, loc in zip(halves(c), local(c)):
            ub[0:pre, loc] = ucarry_ref[:, cols]
            ub[pre:pre + rows, loc] = _dot(xn, wup_ref[:, cols])

    for c in range(min(UP_AHEAD, n_chunks)):
        up(c)

    def finish(live=None, between=()):
        f = None
        for c in range(n_chunks):
            if c + UP_AHEAD < n_chunks:
                up(c + UP_AHEAD)
            ub = ubufs[c % len(ubufs)]
            conv = []
            for cols, loc in zip(halves(c), local(c)):
                ue = ub[:, loc]
                tail = ue[rows:rows + pre, :]
                ucarry_ref[:, cols] = tail if live is None else jnp.where(live, tail, ue[0:pre, :])
                acc = cb_ref[:, cols] + pltpu.roll(ue, 2 * step, 0)[pre:pre + rows, :] * cw_ref[0:1, cols]
                acc = acc + pltpu.roll(ue, step, 0)[pre:pre + rows, :] * cw_ref[1:2, cols]
                conv.append(acc + ue[pre:pre + rows, :] * cw_ref[2:3, cols])
            a = (_gelu_tanh(conv[0]) * conv[1]).astype(BF16)
            part = _dot(a, wdown_ref[bounds[c]:bounds[c + 1], :])
            f = part if f is None else f + part
            if c < len(between):
                between[c]()
        return h + f

    return finish


def _ple(h, norm_g, p, wg_ref, wp_ref):
    xn = _rmsnorm(h, norm_g).astype(BF16)
    gate = _sigmoid(_dot(xn, wg_ref[...]))
    return h + gate * _dot(p.astype(BF16), wp_ref[...])


def _q_heads(h, norm_g, wq_ref, bd_ref, qnw):
    xn = _rmsnorm(h, norm_g).astype(BF16)
    q = _dot(xn, wq_ref[...])
    ssq = _dot((q * q).astype(BF16), bd_ref[...])
    return q * lax.rsqrt(ssq * (1.0 / HEAD_DIM) + EPS) * (qnw * HEAD_DIM ** -0.5)


def _banded_attention_pieces(h, first_tile, refs, store_result, rows):
    (kprev_ref, kcur_ref, vprev_ref, vcur_ref, nmix_ref, wq_ref, bd_ref, qnw_ref, bias_ref,
     sink_ref, wo_ref, klo_ref, khi_ref, vlo_ref, vhi_ref, o_ref) = refs
    low = lax.broadcasted_iota(jnp.int32, (1, LANES), 1) < HEAD_DIM
    top = lax.broadcasted_iota(jnp.int32, (LANES, 1), 0) < HEAD_DIM
    units = [(qb, pr) for qb in range(rows // BLOCK) for pr in range(N_KV // 2)]
    live = {}

    def scores(qb, pr):
        qrows = slice(qb * BLOCK, (qb + 1) * BLOCK)
        krows = slice(qb * BLOCK, (qb + 2) * BLOCK)
        qn = live["qn"]
        qs = jnp.concatenate(
            [qn[qrows, g * KV_DIM + pr * LANES:g * KV_DIM + (pr + 1) * LANES] for g in range(GROUP)],
            axis=0)
        kc = jnp.concatenate([klo_ref[pr, krows, :], khi_ref[pr, krows, :]], axis=0)
        return _dot_nt(kc, qs)

    def project():
        live["qn"] = _q_heads(h, nmix_ref[...], wq_ref, bd_ref, qnw_ref[...]).astype(BF16)
        kw = jnp.concatenate([kprev_ref[...], kcur_ref[...]], axis=0)
        vwt = jnp.concatenate([vprev_ref[...], vcur_ref[...]], axis=0).T
        ones_top = jnp.broadcast_to(jnp.where(top, 1.0, 0.0), (LANES, rows + BLOCK))
        for pr in range(N_KV // 2):
            lanes = slice(pr * LANES, (pr + 1) * LANES)
            klo_ref[pr] = jnp.where(low, kw[:, lanes], 0.0).astype(BF16)
            khi_ref[pr] = jnp.where(low, 0.0, kw[:, lanes]).astype(BF16)
            vlo_ref[pr] = jnp.concatenate([jnp.where(top, vwt[lanes, :], 0.0), ones_top],
                                          axis=0).astype(BF16)
            vhi_ref[pr] = jnp.concatenate([jnp.where(top, 0.0, vwt[lanes, :]), 1.0 - ones_top],
                                          axis=0).astype(BF16)
        live["s"] = scores(*units[0])

    def unit(n):
        qb, pr = units[n]
        s = live["s"]
        if n + 1 < len(units):
            live["s"] = scores(*units[n + 1])
        qrows = slice(qb * BLOCK, (qb + 1) * BLOCK)
        krows = slice(qb * BLOCK, (qb + 2) * BLOCK)
        table = jnp.where(first_tile, 0, 1) if qb == 0 else 1
        es, sinkterms = [], []
        for half in range(2):
            kv = 2 * pr + half
            bias = jnp.concatenate([bias_ref[table, kv * GROUP + g] for g in range(GROUP)], axis=1)
            sh = s[half * 2 * BLOCK:(half + 1) * 2 * BLOCK, :] + bias
            sink = sink_ref[kv]
            m = jnp.maximum(jnp.max(sh, axis=0, keepdims=True), sink)
            es.append(jnp.exp(sh - m).astype(BF16))
            sinkterms.append(jnp.exp(sink - m))
        vct = jnp.concatenate([vlo_ref[pr, :, krows], vhi_ref[pr, :, krows]], axis=1)
        ov = _dot(vct, jnp.concatenate(es, axis=0))
        denom = ov[LANES:, :] + jnp.where(top, sinkterms[0], sinkterms[1])
        o = (ov[:LANES, :] * (1.0 / denom)).T
        for g in range(GROUP):
            o_ref[qrows, g * KV_DIM + pr * LANES:g * KV_DIM + (pr + 1) * LANES] = (
                o[g * BLOCK:(g + 1) * BLOCK, :].astype(BF16))

    def output():
        store_result(h + _dot(o_ref[...], wo_ref[...]))

    return [project] + [functools.partial(unit, n) for n in range(len(units))] + [output]


def _pipeline_flags(n_tiles, total):
    s = pl.program_id(0)
    mixer_live = s < total
    ffn_live = s >= 1
    mixer_first = lax.rem(s, n_tiles) == 0
    ffn_first = lax.rem(s + n_tiles - 1, n_tiles) == 0
    return s, mixer_live, ffn_live, mixer_first, ffn_first


def _pool_prompt_kernel(h_ref, p_ref, nmix_ref, wpool_ref, pscale_ref,
                        nffn_ref, nple_ref, wup_ref, cw_ref, cb_ref, wdown_ref, wg_ref, wp_ref,
                        hout_ref, pstate_ref, cstate_ref,
                        pcarry_ref, ucarry_ref, *ubufs, rows):
    t = pl.program_id(1)
    pre_p, pre_c = PROMPT_POOL_PRE, PROMPT_CONV_PRE

    @pl.when(t == 0)
    def _():
        pcarry_ref[...] = jnp.zeros((pre_p, D_MODEL), F32)
        ucarry_ref[...] = jnp.zeros((pre_c, F2), F32)

    pos = t * rows + lax.broadcasted_iota(jnp.int32, (rows, POOL_GROUP), 0)
    inv_cnt = [1.0 / jnp.minimum(pos + 1, w).astype(F32) for w in POOL_WINDOWS]
    h, state, carry = _pool_stage(h_ref[...], nmix_ref[...], pcarry_ref[...], wpool_ref,
                                  pscale_ref[...], inv_cnt, pre_p, 1, rows)
    pstate_ref[...] = state
    pcarry_ref[...] = carry
    h = _ffn_begin(h, nffn_ref[...], ucarry_ref, list(ubufs), wup_ref, cw_ref, cb_ref, wdown_ref,
                   pre_c, 1, rows)()
    cstate_ref[...] = ucarry_ref[pre_c - CONV_BUF:pre_c, :]
    hout_ref[...] = _ple(h, nple_ref[...], p_ref[...], wg_ref, wp_ref)


def _attn_prompt_kernel(h_ref, p_ref, kprev_ref, kcur_ref, vprev_ref, vcur_ref,
                        nmix_ref, wq_ref, bd_ref, qnw_ref, bias_ref, sink_ref, wo_ref,
                        nffn_ref, nple_ref, wup_ref, cw_ref, cb_ref, wdown_ref, wg_ref, wp_ref,
                        hout_ref, cstate_ref,
                        hmid_ref, klo_ref, khi_ref, vlo_ref, vhi_ref, o_ref, ucarry_ref, *ubufs,
                        rows, n_tiles, total):
    s, _, ffn_live, mixer_first, ffn_first = _pipeline_flags(n_tiles, total)
    pre_c = PROMPT_CONV_PRE

    @pl.when(s == 0)
    def _():
        hmid_ref[...] = jnp.zeros(hmid_ref.shape, F32)

    @pl.when(jnp.logical_or(ffn_first, s == 0))
    def _():
        ucarry_ref[...] = jnp.zeros((pre_c, F2), F32)

    slot = lax.rem(s, 2)
    finish = _ffn_begin(hmid_ref[1 - slot], nffn_ref[...], ucarry_ref, list(ubufs), wup_ref, cw_ref,
                        cb_ref, wdown_ref, pre_c, 1, rows)

    def store_result(value):
        hmid_ref[slot] = value

    pieces = _banded_attention_pieces(
        h_ref[...], mixer_first,
        (kprev_ref, kcur_ref, vprev_ref, vcur_ref, nmix_ref, wq_ref, bd_ref, qnw_ref, bias_ref,
         sink_ref, wo_ref, klo_ref, khi_ref, vlo_ref, vhi_ref, o_ref), store_result, rows)
    n_between = -(-D_FF // FF_CHUNK) - 1
    lead = max(len(pieces) - n_between, 1)
    for piece in pieces[:lead]:
        piece()
    h = finish(ffn_live, pieces[lead:])
    cstate_ref[...] = ucarry_ref[pre_c - CONV_BUF:pre_c, :]
    hout_ref[...] = _ple(h, nple_ref[...], p_ref[...], wg_ref, wp_ref)


def _decode_ffn_ple(h, refs, pre_c, step, rows):
    (p_ref, cpre_ref, nffn_ref, nple_ref, wup_ref, cw_ref, cb_ref, wdown_ref, wg_ref, wp_ref,
     hout_ref, cstate_ref, ucarry_ref, ubufs) = refs
    ucarry_ref[...] = cpre_ref[...].reshape(pre_c, F2)
    h = _ffn_begin(h, nffn_ref[...], ucarry_ref, ubufs, wup_ref, cw_ref, cb_ref, wdown_ref,
                   pre_c, step, rows)()
    cstate_ref[...] = ucarry_ref[...].reshape(cstate_ref.shape)
    p = p_ref[...].reshape(rows, PLE_DIM)
    hout_ref[...] = _ple(h, nple_ref[...], p, wg_ref, wp_ref).reshape(hout_ref.shape)


def _pool_decode_kernel(h_ref, p_ref, ppre_ref, cpre_ref, nmix_ref, wpool_ref, pscale_ref,
                        nffn_ref, nple_ref, wup_ref, cw_ref, cb_ref, wdown_ref, wg_ref, wp_ref,
                        hout_ref, pstate_ref, cstate_ref, ucarry_ref, *ubufs, rows, step):
    pre_p, pre_c = POOL_BUF * step, CONV_BUF * step
    inv_cnt = [1.0 / min(PAST_LEN + 1, w) for w in POOL_WINDOWS]
    h, state, _ = _pool_stage(h_ref[...].reshape(rows, D_MODEL), nmix_ref[...],
                              ppre_ref[...].reshape(pre_p, D_MODEL), wpool_ref, pscale_ref[...],
                              inv_cnt, pre_p, step, rows)
    pstate_ref[...] = state.reshape(pstate_ref.shape)
    _decode_ffn_ple(h, (p_ref, cpre_ref, nffn_ref, nple_ref, wup_ref, cw_ref, cb_ref, wdown_ref,
                        wg_ref, wp_ref, hout_ref, cstate_ref, ucarry_ref, list(ubufs)),
                    pre_c, step, rows)


def _attn_post_decode_kernel(h_ref, o_ref, p_ref, cpre_ref, wo_ref,
                             nffn_ref, nple_ref, wup_ref, cw_ref, cb_ref, wdown_ref, wg_ref, wp_ref,
                             hout_ref, cstate_ref, ucarry_ref, *ubufs, rows, step):
    h = h_ref[...].reshape(rows, D_MODEL) + _dot(o_ref[...].reshape(rows, D_MODEL).astype(BF16),
                                                 wo_ref[...])
    _decode_ffn_ple(h, (p_ref, cpre_ref, nffn_ref, nple_ref, wup_ref, cw_ref, cb_ref, wdown_ref,
                        wg_ref, wp_ref, hout_ref, cstate_ref, ucarry_ref, list(ubufs)),
                    CONV_BUF * step, step, rows)


def _kv_project(h, norm_ref, wk_ref, wv_ref, bd_ref, knw_ref, k_ref, v_ref):
    xn = _rmsnorm(h, norm_ref[...]).astype(BF16)
    k = _dot(xn, wk_ref[...])
    ssq = _dot((k * k).astype(BF16), bd_ref[...])
    k_ref[...] = k * lax.rsqrt(ssq * (1.0 / HEAD_DIM) + EPS) * knw_ref[...]
    v_ref[...] = _dot(xn, wv_ref[...])


def _kv_kernel(h_ref, *refs):
    _kv_project(h_ref[...], *refs)


def _softmax_sink(s, sink):
    m = jnp.maximum(jnp.max(s, axis=-1, keepdims=True), sink)
    e = jnp.exp(s - m)
    denom = jnp.sum(e, axis=-1, keepdims=True) + jnp.exp(sink - m)
    return e * (1.0 / denom)


def _decode_attn_kernel(h_ref, knew_ref, vnew_ref, ck_ref, cv_ref,
                        nmix_ref, wq_ref, bd_ref, qnw_ref, bias_ref, sink_ref, kvmask_ref, rowa_ref,
                        o_ref, qn_ref, kxa_ref, kxb_ref, vxa_ref, vxb_ref, *, kpad):
    c = pl.program_id(0)

    @pl.when(c == 0)
    def _():
        qn_ref[...] = _q_heads(h_ref[...], nmix_ref[...], wq_ref, bd_ref, qnw_ref[...])
        zeros = jnp.zeros((kpad - WINDOW - PAIR_ROWS, KV_DIM), F32)
        for ref in (kxa_ref, kxb_ref, vxa_ref, vxb_ref):
            ref[WINDOW + PAIR_ROWS:kpad, :] = zeros

    kvmask = kvmask_ref[...]
    row_a = rowa_ref[...]
    row_b = 1.0 - row_a
    bias = bias_ref[...]
    sink = sink_ref[...]

    def rows_of(i):
        return pl.ds(pl.multiple_of((c * (SEQ_CHUNK // 2) + i) * PAIR_ROWS, PAIR_ROWS), PAIR_ROWS)

    def extended(ref, cache, new, seq):
        ref[0:WINDOW, :] = cache[seq]
        ref[WINDOW:WINDOW + PAIR_ROWS, :] = new
        return ref[...].astype(BF16)

    def scores(i):
        q8 = qn_ref[rows_of(i), :]
        qg = jnp.concatenate([q8[:, g * KV_DIM:(g + 1) * KV_DIM] for g in range(GROUP)], axis=0)
        qb = (jnp.concatenate([qg] * N_KV, axis=0) * kvmask).astype(BF16)
        knew = knew_ref[rows_of(i), :]
        sa = _dot_nt(qb, extended(kxa_ref, ck_ref, knew, 2 * i))
        sb = _dot_nt(qb, extended(kxb_ref, ck_ref, knew, 2 * i + 1))
        return sa * row_a + sb * row_b + bias

    nxt = scores(0)
    for i in range(SEQ_CHUNK // 2):
        s = nxt
        if i + 1 < SEQ_CHUNK // 2:
            nxt = scores(i + 1)
        pr = _softmax_sink(s, sink)
        vnew = vnew_ref[rows_of(i), :]
        o = (_dot((pr * row_a).astype(BF16), extended(vxa_ref, cv_ref, vnew, 2 * i))
             + _dot((pr * row_b).astype(BF16), extended(vxb_ref, cv_ref, vnew, 2 * i + 1))) * kvmask
        blk = GROUP * PAIR_ROWS
        og = o[0:blk] + o[blk:2 * blk] + o[2 * blk:3 * blk] + o[3 * blk:4 * blk]
        for g in range(GROUP):
            o_ref[rows_of(i), g * KV_DIM:(g + 1) * KV_DIM] = og[g * PAIR_ROWS:(g + 1) * PAIR_ROWS]


def _resident(shape, layer=None):
    zeros = (0,) * len(shape)
    if layer is None:
        return pl.BlockSpec(shape, lambda *_: zeros, pipeline_mode=pl.Buffered(1))
    return pl.BlockSpec((None,) + shape, lambda *_: (layer,) + zeros, pipeline_mode=pl.Buffered(1))


def _params(n_axes):
    return pltpu.CompilerParams(dimension_semantics=("arbitrary",) * n_axes,
                                vmem_limit_bytes=VMEM_LIMIT_BYTES)


def _ffn_weight_specs(i):
    return [_resident((1, D_MODEL), i), _resident((1, D_MODEL), i), _resident((D_MODEL, F2), i),
            _resident((CONV_W, F2), i), _resident((1, F2), i), _resident((D_FF, D_MODEL), i),
            _resident((D_MODEL, D_MODEL), i), _resident((PLE_DIM, D_MODEL), i)]


def _ffn_weights(W):
    return [W["norm_ffn"], W["norm_ple"], W["w_up"], W["conv_w"], W["conv_b"], W["w_down"],
            W["w_ple_gate"], W["w_ple_proj"]]


def _ffn_scratch(pre_c, rows):
    return ([pltpu.VMEM((pre_c, F2), F32)]
            + [pltpu.VMEM((pre_c + rows, 2 * FF_CHUNK), F32) for _ in range(UP_AHEAD + 1)])


def _prompt_specs(rows, n_tiles, total):
    mixer = lambda s: jnp.minimum(s, total - 1)
    ffn = lambda s: jnp.maximum(s - 1, 0)

    def make(which, per_sequence):
        def build(r, width, layer=None):
            lead = () if layer is None else (layer,)
            index = lambda s: lead + (which(s) // n_tiles, 0 if per_sequence else which(s) % n_tiles, 0)
            return pl.BlockSpec((None,) * (len(lead) + 1) + (r, width), index)
        return build

    return make(mixer, False), make(ffn, False), make(mixer, True), make(ffn, True)


def _kv_weight_specs():
    return [_resident((1, D_MODEL)), _resident((D_MODEL, KV_DIM)), _resident((D_MODEL, KV_DIM)),
            _resident((KV_DIM, KV_DIM)), _resident((1, KV_DIM))]


def _kv_weights(W, A):
    return [W["kv_norm"], A["w_k"], A["w_v"], A["bd_k"], A["knw"]]


def _pool_prompt_layer(h, p, W, i, *, rows):
    batch, length, _ = h.shape
    assert length % rows == 0
    tile = lambda width: pl.BlockSpec((None, rows, width), lambda b, t: (b, t, 0))
    state = lambda r, width: pl.BlockSpec((None, r, width), lambda b, t: (b, 0, 0))
    in_specs = [tile(D_MODEL), pl.BlockSpec((None, None, rows, PLE_DIM), lambda b, t: (i, b, t, 0)),
                _resident((1, D_MODEL), i),
                _resident((len(POOL_WINDOWS), POOL_GROUP, POOL_GROUP), i),
                _resident((1, D_MODEL), i)] + _ffn_weight_specs(i)
    args = [h, p, W["norm_mix"], W["w_pool"], W["pool_scale"]] + _ffn_weights(W)
    out_specs = [tile(D_MODEL), state(POOL_BUF, D_MODEL), state(CONV_BUF, F2)]
    out_shape = [jax.ShapeDtypeStruct(h.shape, F32),
                 jax.ShapeDtypeStruct((batch, POOL_BUF, D_MODEL), F32),
                 jax.ShapeDtypeStruct((batch, CONV_BUF, F2), F32)]
    kern = functools.partial(_pool_prompt_kernel, rows=rows)
    return pl.pallas_call(
        kern,
        grid=(batch, length // rows),
        in_specs=in_specs,
        out_specs=out_specs,
        out_shape=out_shape,
        scratch_shapes=[pltpu.VMEM((PROMPT_POOL_PRE, D_MODEL), F32)]
        + _ffn_scratch(PROMPT_CONV_PRE, rows),
        compiler_params=_params(2),
        name="pool_layer_prompt",
    )(*args)


def _attn_prompt_layer(h, p, k, v, W, A, i, *, rows):
    batch, length, _ = h.shape
    j = i - N_A
    n_tiles = length // rows
    total = batch * n_tiles
    blocks = rows // BLOCK
    assert n_tiles > 1 and length % rows == 0
    mixer_tile, ffn_tile, _, ffn_state = _prompt_specs(rows, n_tiles, total)
    mixer = lambda s: jnp.minimum(s, total - 1)
    prev = pl.BlockSpec(
        (None, BLOCK, KV_DIM),
        lambda s: (mixer(s) // n_tiles, jnp.maximum((mixer(s) % n_tiles) * blocks - 1, 0), 0))
    in_specs = [mixer_tile(rows, D_MODEL), ffn_tile(rows, PLE_DIM, i),
                prev, mixer_tile(rows, KV_DIM), prev, mixer_tile(rows, KV_DIM),
                _resident((1, D_MODEL), i), _resident((D_MODEL, D_MODEL), j),
                _resident((D_MODEL, D_MODEL)), _resident((1, D_MODEL), j),
                _resident((2, N_HEADS, 2 * BLOCK, BLOCK)),
                _resident((N_KV, 1, GROUP * BLOCK), j), _resident((D_MODEL, D_MODEL), j)]
    in_specs += _ffn_weight_specs(i)
    args = [h, p, k, k, v, v, W["norm_mix"], A["w_q"], A["bd_q"], A["qnw"],
            A["bias_prompt"], A["sink_prompt"], A["w_o"]] + _ffn_weights(W)
    kern = functools.partial(_attn_prompt_kernel, rows=rows, n_tiles=n_tiles, total=total)
    khalf = pltpu.VMEM((N_KV // 2, rows + BLOCK, LANES), BF16)
    vhalf = pltpu.VMEM((N_KV // 2, 2 * LANES, rows + BLOCK), BF16)
    return pl.pallas_call(
        kern,
        grid=(total + 1,),
        in_specs=in_specs,
        out_specs=[ffn_tile(rows, D_MODEL), ffn_state(CONV_BUF, F2)],
        out_shape=[jax.ShapeDtypeStruct(h.shape, F32),
                   jax.ShapeDtypeStruct((batch, CONV_BUF, F2), F32)],
        scratch_shapes=[pltpu.VMEM((2, rows, D_MODEL), F32), khalf, khalf, vhalf, vhalf,
                        pltpu.VMEM((rows, D_MODEL), BF16)] + _ffn_scratch(PROMPT_CONV_PRE, rows),
        compiler_params=_params(1),
        name="attn_layer_prompt",
    )(*args)


def _decode_specs(rows, step):
    def build(r, width, layer=None):
        lead = () if layer is None else (layer,)
        return pl.BlockSpec((None,) * len(lead) + (r, step, width), lambda t: lead + (0, t, 0))
    return build


def _pool_decode_layer(h, p, ppre, cpre, W, i, *, rows, step):
    steps, n_seq, _ = h.shape
    block = _decode_specs(rows, step)
    in_specs = [block(steps, D_MODEL), block(steps, PLE_DIM, i), block(POOL_BUF, D_MODEL, i),
                block(CONV_BUF, F2, i), _resident((1, D_MODEL), i),
                _resident((len(POOL_WINDOWS), POOL_GROUP, POOL_GROUP), i),
                _resident((1, D_MODEL), i)] + _ffn_weight_specs(i)
    kern = functools.partial(_pool_decode_kernel, rows=rows, step=step)
    return pl.pallas_call(
        kern,
        grid=(n_seq // step,),
        in_specs=in_specs,
        out_specs=[block(steps, D_MODEL), block(POOL_BUF, D_MODEL), block(CONV_BUF, F2)],
        out_shape=[jax.ShapeDtypeStruct(h.shape, F32),
                   jax.ShapeDtypeStruct((POOL_BUF, n_seq, D_MODEL), F32),
                   jax.ShapeDtypeStruct((CONV_BUF, n_seq, F2), F32)],
        scratch_shapes=_ffn_scratch(CONV_BUF * step, rows),
        compiler_params=_params(1),
        name="pool_layer_decode",
    )(h, p, ppre, cpre, W["norm_mix"], W["w_pool"], W["pool_scale"], *_ffn_weights(W))


def _attn_post_decode_layer(h, o, p, cpre, W, A, i, *, rows, step):
    steps, n_seq, _ = h.shape
    j = i - N_A
    block = _decode_specs(rows, step)
    kern = functools.partial(_attn_post_decode_kernel, rows=rows, step=step)
    return pl.pallas_call(
        kern,
        grid=(n_seq // step,),
        in_specs=[block(steps, D_MODEL), block(steps, D_MODEL), block(steps, PLE_DIM, i),
                  block(CONV_BUF, F2, i), _resident((D_MODEL, D_MODEL), j)] + _ffn_weight_specs(i),
        out_specs=[block(steps, D_MODEL), block(CONV_BUF, F2)],
        out_shape=[jax.ShapeDtypeStruct(h.shape, F32),
                   jax.ShapeDtypeStruct((CONV_BUF, n_seq, F2), F32)],
        scratch_shapes=_ffn_scratch(CONV_BUF * step, rows),
        compiler_params=_params(1),
        name="attn_post_layer_decode",
    )(h, o, p, cpre, A["w_o"], *_ffn_weights(W))


def _shared_kv(h2d, W, A, *, rows):
    total = h2d.shape[0]
    tile = lambda width: pl.BlockSpec((rows, width), lambda t: (t, 0))
    return pl.pallas_call(
        _kv_kernel,
        grid=(total // rows,),
        in_specs=[tile(D_MODEL)] + _kv_weight_specs(),
        out_specs=[tile(KV_DIM), tile(KV_DIM)],
        out_shape=[jax.ShapeDtypeStruct((total, KV_DIM), F32)] * 2,
        compiler_params=_params(1),
        name="shared_kv",
    )(h2d, *_kv_weights(W, A))


def _decode_attn(h_bm, knew, vnew, cache_k, cache_v, W, A, i):
    rows = h_bm.shape[0]
    n_seq = cache_k.shape[0]
    j = i - N_A
    kpad = A["bias_decode"].shape[1]
    qrows = N_KV * GROUP * PAIR_ROWS
    full = lambda r, width: pl.BlockSpec((r, width), lambda c: (0, 0))
    cache = pl.BlockSpec((SEQ_CHUNK, WINDOW, KV_DIM), lambda c: (c, 0, 0))
    kern = functools.partial(_decode_attn_kernel, kpad=kpad)
    ext = pltpu.VMEM((kpad, KV_DIM), F32)
    return pl.pallas_call(
        kern,
        grid=(n_seq // SEQ_CHUNK,),
        in_specs=[full(rows, D_MODEL), full(rows, KV_DIM), full(rows, KV_DIM), cache, cache,
                  _resident((1, D_MODEL), i), _resident((D_MODEL, D_MODEL), j),
                  _resident((D_MODEL, D_MODEL)), _resident((1, D_MODEL), j),
                  _resident((qrows, kpad)), _resident((qrows, 1), j),
                  _resident((qrows, KV_DIM)), _resident((qrows, 1))],
        out_specs=full(rows, D_MODEL),
        out_shape=jax.ShapeDtypeStruct((rows, D_MODEL), F32),
        scratch_shapes=[pltpu.VMEM((rows, D_MODEL), F32), ext, ext, ext, ext],
        compiler_params=_params(1),
        name="decode_attn",
    )(h_bm, knew, vnew, cache_k, cache_v, W["norm_mix"], A["w_q"], A["bd_q"], A["qnw"],
      A["bias_decode"], A["sink_decode"], A["kvmask"], A["row_a"])


def _t5_bucket(d):
    n = jnp.maximum(d, 0)
    max_exact = N_BUCKETS // 2
    nf = jnp.maximum(n, 1).astype(F32)
    large = max_exact + (jnp.log(nf / max_exact) / math.log(MAX_DISTANCE / max_exact)
                         * (N_BUCKETS - max_exact)).astype(jnp.int32)
    large = jnp.minimum(large, N_BUCKETS - 1)
    return jnp.where(n < max_exact, n, large)


def _block_diag_ones(width):
    idx = np.arange(width) // HEAD_DIM
    return jnp.asarray(idx[:, None] == idx[None, :], BF16)


def _banded_bias(tab):
    heads = tab.shape[1]
    period = 3 * BLOCK + 1
    base = jnp.concatenate([jnp.full((heads, BLOCK - 1), NEG_INF, F32), tab.T,
                            jnp.full((heads, period - 2 * BLOCK + 1), NEG_INF, F32)], axis=1)
    flat = jnp.tile(base, (1, 2 * BLOCK))[:, :2 * BLOCK * (period - 1)]
    skew = flat.reshape(heads, 2 * BLOCK, period - 1)
    return skew[:, :, 2 * BLOCK - 1:3 * BLOCK - 1]


def _attention_tables(rel_bias, sinks):
    A = {}
    tab = rel_bias[_t5_bucket(jnp.arange(WINDOW))].astype(F32)

    steady = _banded_bias(tab)
    has_prev = (np.arange(2 * BLOCK) >= BLOCK)[None, :, None]
    A["bias_prompt"] = jnp.stack([jnp.where(has_prev, steady, NEG_INF), steady])

    tab3 = tab.reshape(WINDOW, N_KV, GROUP)
    steps = PAIR_ROWS // 2
    kpad = -(-(WINDOW + PAIR_ROWS) // 16) * 16
    col = np.arange(kpad)[None, :]
    r = np.arange(PAIR_ROWS)[:, None]
    step_of = r % steps
    seq_b = r >= steps
    keyidx = np.where(col < WINDOW, col, WINDOW + (col - WINDOW) % steps)
    own = (col < WINDOW) | ((col < WINDOW + PAIR_ROWS) & (((col - WINDOW) >= steps) == seq_b))
    dd = step_of + WINDOW - keyidx
    dvalid = own & (dd >= 0) & (dd < WINDOW)
    ddc = np.clip(dd, 0, WINDOW - 1)
    dec = jnp.where(dvalid[:, :, None, None], tab3[ddc], NEG_INF)
    A["bias_decode"] = dec.transpose(2, 3, 0, 1).reshape(N_KV * GROUP * PAIR_ROWS, kpad)
    A["row_a"] = jnp.asarray(np.tile(~seq_b, (N_KV * GROUP, 1)).reshape(-1, 1), F32)
    lane_kv = np.arange(KV_DIM)[None, :] // HEAD_DIM
    row_kv = np.arange(N_KV * GROUP * PAIR_ROWS)[:, None] // (GROUP * PAIR_ROWS)
    A["kvmask"] = jnp.asarray(lane_kv == row_kv, F32)

    sk = sinks.astype(F32).reshape(-1, N_KV, GROUP)
    A["sink_prompt"] = jnp.repeat(sk, BLOCK, axis=2)[:, :, None, :]
    A["sink_decode"] = jnp.repeat(sk.reshape(sk.shape[0], -1), PAIR_ROWS, axis=1)[..., None]
    return A


def _prepare(W):
    A = _attention_tables(W["rel_bias"], W["sinks"])
    n_b = W["w_q"].shape[0]
    A["w_q"] = (W["w_q"].reshape(n_b, D_MODEL, N_KV, GROUP, HEAD_DIM).transpose(0, 1, 3, 2, 4)
                .reshape(n_b, D_MODEL, D_MODEL).astype(BF16))
    A["w_o"] = (W["w_o"].reshape(n_b, N_KV, GROUP, HEAD_DIM, D_MODEL).transpose(0, 2, 1, 3, 4)
                .reshape(n_b, D_MODEL, D_MODEL).astype(BF16))
    A["qnw"] = jnp.tile(W["q_norm"], (1, N_HEADS)).reshape(n_b, 1, D_MODEL)
    A["knw"] = jnp.tile(W["k_norm"], N_KV).reshape(1, KV_DIM)
    A["w_k"] = W["w_k"].astype(BF16)
    A["w_v"] = W["w_v"].astype(BF16)
    A["bd_q"] = _block_diag_ones(D_MODEL)
    A["bd_k"] = _block_diag_ones(KV_DIM)
    V = dict(W)
    for name in ("w_pool", "w_up", "w_down", "w_ple_gate", "w_ple_proj"):
        V[name] = W[name].astype(BF16)
    for name in ("norm_mix", "norm_ffn", "norm_ple", "pool_scale", "conv_b"):
        V[name] = W[name][:, None, :]
    V["kv_norm"] = W["kv_norm"][None, :]
    return V, A


def kernel(x_prompt, x_sample, p_prompt, p_sample, state_pool, state_conv, cache_k, cache_v,
           norm_mix, norm_ffn, norm_ple, w_pool, pool_scale, kv_norm, w_k, w_v, k_norm,
           w_q, q_norm, sinks, w_o, rel_bias, w_up, conv_w, conv_b, w_down, w_ple_gate, w_ple_proj):
    W, A = _prepare(dict(
        norm_mix=norm_mix, norm_ffn=norm_ffn, norm_ple=norm_ple, w_pool=w_pool,
        pool_scale=pool_scale, kv_norm=kv_norm, w_k=w_k, w_v=w_v, k_norm=k_norm, w_q=w_q,
        q_norm=q_norm, sinks=sinks, w_o=w_o, rel_bias=rel_bias, w_up=w_up, conv_w=conv_w,
        conv_b=conv_b, w_down=w_down, w_ple_gate=w_ple_gate, w_ple_proj=w_ple_proj))
    batch, seq, _ = x_prompt.shape
    n_seq, steps, _ = x_sample.shape

    h = x_prompt
    pool_p, conv_p = [], []
    for i in range(N_A):
        h, ps, cs = _pool_prompt_layer(h, p_prompt, W, i, rows=PROMPT_TILE)
        pool_p.append(ps)
        conv_p.append(cs)
    k_p, v_p = _shared_kv(h.reshape(batch * seq, D_MODEL), W, A, rows=KV_TILE)
    k_p = k_p.reshape(batch, seq, KV_DIM)
    v_p = v_p.reshape(batch, seq, KV_DIM)
    for i in range(N_A, DEPTH):
        h, cs = _attn_prompt_layer(h, p_prompt, k_p, v_p, W, A, i, rows=PROMPT_TILE)
        conv_p.append(cs)
    y_prompt = h

    swap = lambda x: jnp.swapaxes(x, -3, -2)
    tile_rows = steps * DECODE_SEQ_TILE
    total = steps * n_seq
    hs = swap(x_sample)
    ps_sw, ppre_sw, cpre_sw = swap(p_sample), swap(state_pool), swap(state_conv)
    pool_s, conv_s = [], []
    for i in range(N_A):
        hs, ps, cs = _pool_decode_layer(hs, ps_sw, ppre_sw, cpre_sw, W, i, rows=tile_rows,
                                        step=DECODE_SEQ_TILE)
        pool_s.append(ps)
        conv_s.append(cs)
    hs_bm = swap(hs).reshape(total, D_MODEL)
    k_new, v_new = _shared_kv(hs_bm, W, A, rows=total)
    ck = cache_k.reshape(n_seq, WINDOW, KV_DIM)
    cv = cache_v.reshape(n_seq, WINDOW, KV_DIM)
    for i in range(N_A, DEPTH):
        o_bm = _decode_attn(hs_bm, k_new, v_new, ck, cv, W, A, i)
        hs, cs = _attn_post_decode_layer(hs, swap(o_bm.reshape(n_seq, steps, D_MODEL)), ps_sw,
                                         cpre_sw, W, A, i, rows=tile_rows, step=DECODE_SEQ_TILE)
        conv_s.append(cs)
        hs_bm = swap(hs).reshape(total, D_MODEL)
    y_sample = hs_bm.reshape(n_seq, steps, D_MODEL)

    def window(cache, new):
        ext = jnp.concatenate([cache, new.reshape(n_seq, steps, N_KV, HEAD_DIM)], axis=1)
        return ext[:, -WINDOW:]

    kp4 = k_p[:, -WINDOW:].reshape(batch, WINDOW, N_KV, HEAD_DIM)
    vp4 = v_p[:, -WINDOW:].reshape(batch, WINDOW, N_KV, HEAD_DIM)
    return (y_prompt, y_sample, jnp.stack(pool_p), swap(jnp.stack(pool_s)), jnp.stack(conv_p),
            swap(jnp.stack(conv_s)), kp4, window(cache_k, k_new), vp4, window(cache_v, v_new))
```

```python
import functools
import math

import numpy as np
import jax
import jax.numpy as jnp
from jax import lax
from jax.experimental import pallas as pl
from jax.experimental.pallas import tpu as pltpu

D_MODEL = 1024
DEPTH = 4
N_A = DEPTH // 2
POOL_WINDOWS = (2, 4, 8, 16)
POOL_GROUP = D_MODEL // len(POOL_WINDOWS)
POOL_BUF = max(POOL_WINDOWS) - 1
HEAD_DIM = 64
N_HEADS = D_MODEL // HEAD_DIM
N_KV = 4
GROUP = N_HEADS // N_KV
KV_DIM = N_KV * HEAD_DIM
WINDOW = 128
BLOCK = WINDOW
N_BUCKETS = 32
MAX_DISTANCE = 128
D_FF = 2816
F2 = 2 * D_FF
CONV_W = 3
CONV_BUF = CONV_W - 1
PLE_DIM = 256
EPS = 1e-6
PAST_LEN = 8192

LANES = 128
SUBLANES = 8
VMEM_LIMIT_BYTES = 58 * 1024 * 1024

PROMPT_TILE = 256
KV_TILE = 1024
FF_CHUNK = 512
UP_AHEAD = 2
ROW_PHASES = 4
DECODE_SEQ_TILE = 64
SEQ_CHUNK = 16
PAIR_ROWS = 2 * 4
PROMPT_POOL_PRE = 2 * SUBLANES
PROMPT_CONV_PRE = SUBLANES

F32 = jnp.float32
BF16 = jnp.bfloat16
NEG_INF = float("-inf")


def _rmsnorm(x, g):
    ms = jnp.mean(x * x, axis=-1, keepdims=True)
    return x * lax.rsqrt(ms + EPS) * g


def _gelu_tanh(x):
    c = np.float32(math.sqrt(2.0 / math.pi))
    cdf = 0.5 * (1.0 + jnp.tanh(c * (x + 0.044715 * (x * x * x))))
    return x * cdf


def _sigmoid(x):
    return 1.0 / (1.0 + jnp.exp(-x))


def _dot(a, b):
    return jnp.dot(a, b, preferred_element_type=F32)


def _dot_nt(a, b):
    return lax.dot_general(a, b, (((1,), (1,)), ((), ())), preferred_element_type=F32)


def _pool_stage(h, norm_g, carried, wpool_ref, scale, inv_cnt, pre, step, rows):
    xn = _rmsnorm(h, norm_g)
    ext = jnp.concatenate([carried, xn], axis=0)
    state = ext[pre + rows - POOL_BUF * step:pre + rows, :]
    carry = ext[rows:rows + pre, :]
    outs = []
    for g, w in enumerate(POOL_WINDOWS):
        cols = slice(g * POOL_GROUP, (g + 1) * POOL_GROUP)
        s = ext[:, cols]
        width = 1
        while width < w:
            s = s + pltpu.roll(s, width * step, 0)
            width *= 2
        d = s[pre:pre + rows, :] * inv_cnt[g] - xn[:, cols]
        outs.append(_dot(d.astype(BF16), wpool_ref[g]))
    return h + jnp.concatenate(outs, axis=1) * scale, state, carry


def _ffn_begin(h, norm_g, ucarry_ref, ubufs, wup_ref, cw_ref, cb_ref, wdown_ref, pre, step, rows):
    xn = _rmsnorm(h, norm_g).astype(BF16)
    bounds = list(range(0, D_FF, FF_CHUNK)) + [D_FF]
    n_chunks = len(bounds) - 1
    halves = lambda c: [slice(base + bounds[c], base + bounds[c + 1]) for base in (0, D_FF)]
    local = lambda c: [slice(j * FF_CHUNK, j * FF_CHUNK + bounds[c + 1] - bounds[c]) for j in range(2)]

    def up(c):
        ub = ubufs[c % len(ubufs)]
        for cols, loc in zip(halves(c), local(c)):
            ub[0:pre, loc] = ucarry_ref[:, cols]
            ub[pre:pre + rows, loc] = _dot(xn, wup_ref[:, cols])

    for c in range(min(UP_AHEAD, n_chunks)):
        up(c)

    def finish(live=None, between=()):
        f = None
        for c in range(n_chunks):
            if c + UP_AHEAD < n_chunks:
                up(c + UP_AHEAD)
            ub = ubufs[c % len(ubufs)]
            conv = []
            for cols, loc in zip(halves(c), local(c)):
                ue = ub[:, loc]
                tail = ue[rows:rows + pre, :]
                ucarry_ref[:, cols] = tail if live is None else jnp.where(live, tail, ue[0:pre, :])
                acc = cb_ref[:, cols] + pltpu.roll(ue, 2 * step, 0)[pre:pre + rows, :] * cw_ref[0:1, cols]
                acc = acc + pltpu.roll(ue, step, 0)[pre:pre + rows, :] * cw_ref[1:2, cols]
                conv.append(acc + ue[pre:pre + rows, :] * cw_ref[2:3, cols])
            a = (_gelu_tanh(conv[0]) * conv[1]).astype(BF16)
            part = _dot(a, wdown_ref[bounds[c]:bounds[c + 1], :])
            f = part if f is None else f + part
            if c < len(between):
                between[c]()
        return h + f

    return finish


def _ffn_begin_prompt(h, norm_g, ucarry_ref, ubufs, fs_ref, wup_ref, cw_ref, cb_ref, wdown_ref,
                      pre, rows, live=None):
    xn = _rmsnorm(h, norm_g).astype(BF16)
    bounds = list(range(0, D_FF, FF_CHUNK)) + [D_FF]
    n_chunks = len(bounds) - 1
    slabs_per_half = FF_CHUNK // LANES
    part_rows = rows // ROW_PHASES

    def up(c):
        ub = ubufs[c % len(ubufs)]
        for half, base in enumerate((0, D_FF)):
            cols = slice(base + bounds[c], base + bounds[c + 1])
            u = _dot(xn, wup_ref[:, cols])
            old = ucarry_ref[:, cols]
            tail = u[rows - pre:rows, :]
            ucarry_ref[:, cols] = tail if live is None else jnp.where(live, tail, old)
            for k in range((bounds[c + 1] - bounds[c]) // LANES):
                lanes = slice(k * LANES, (k + 1) * LANES)
                ub[half * slabs_per_half + k, 0:pre, :] = old[:, lanes]
                ub[half * slabs_per_half + k, pre:pre + rows, :] = u[:, lanes]

    for c in range(min(UP_AHEAD, n_chunks)):
        up(c)

    def finish(between=()):
        f = None
        for c in range(n_chunks):
            ub = ubufs[c % len(ubufs)]
            phases = []
            for s in range(ROW_PHASES):
                slabs = []
                for k in range((bounds[c + 1] - bounds[c]) // LANES):
                    conv = []
                    for half, base in enumerate((0, D_FF)):
                        cols = slice(base + bounds[c] + k * LANES, base + bounds[c] + (k + 1) * LANES)
                        slab = half * slabs_per_half + k
                        tap = lambda d: ub[slab, pl.ds(pre + s - d, part_rows, stride=ROW_PHASES), :]
                        acc = cb_ref[:, cols] + tap(2) * cw_ref[0:1, cols]
                        acc = acc + tap(1) * cw_ref[1:2, cols]
                        conv.append(acc + tap(0) * cw_ref[2:3, cols])
                    slabs.append((_gelu_tanh(conv[0]) * conv[1]).astype(BF16))
                phases.append(jnp.concatenate(slabs, axis=1))
            a = jnp.concatenate(phases, axis=0)
            if c + UP_AHEAD < n_chunks:
                up(c + UP_AHEAD)
            part = _dot(a, wdown_ref[bounds[c]:bounds[c + 1], :])
            f = part if f is None else f + part
            if c < len(between):
                between[c]()
        for s in range(ROW_PHASES):
            for k in range(D_MODEL // LANES):
                fs_ref[k, pl.ds(s, part_rows, stride=ROW_PHASES), :] = (
                    f[s * part_rows:(s + 1) * part_rows, k * LANES:(k + 1) * LANES])
        return h + jnp.concatenate([fs_ref[k] for k in range(D_MODEL // LANES)], axis=1)

    return finish


def _ple(h, norm_g, p, wg_ref, wp_ref):
    xn = _rmsnorm(h, norm_g).astype(BF16)
    gate = _sigmoid(_dot(xn, wg_ref[...]))
    return h + gate * _dot(p.astype(BF16), wp_ref[...])


def _q_heads(h, norm_g, wq_ref, bd_ref, qnw):
    xn = _rmsnorm(h, norm_g).astype(BF16)
    q = _dot(xn, wq_ref[...])
    ssq = _dot((q * q).astype(BF16), bd_ref[...])
    return q * lax.rsqrt(ssq * (1.0 / HEAD_DIM) + EPS) * (qnw * HEAD_DIM ** -0.5)


def _banded_attention_pieces(h, first_tile, refs, store_result, rows):
    (kprev_ref, kcur_ref, vprev_ref, vcur_ref, nmix_ref, wq_ref, bd_ref, qnw_ref, bias_ref,
     sink_ref, wo_ref, klo_ref, khi_ref, vlo_ref, vhi_ref, o_ref) = refs
    low = lax.broadcasted_iota(jnp.int32, (1, LANES), 1) < HEAD_DIM
    top = lax.broadcasted_iota(jnp.int32, (LANES, 1), 0) < HEAD_DIM
    units = [(qb, pr) for qb in range(rows // BLOCK) for pr in range(N_KV // 2)]
    live = {}

    def scores(qb, pr):
        qrows = slice(qb * BLOCK, (qb + 1) * BLOCK)
        krows = slice(qb * BLOCK, (qb + 2) * BLOCK)
        qn = live["qn"]
        qs = jnp.concatenate(
            [qn[qrows, g * KV_DIM + pr * LANES:g * KV_DIM + (pr + 1) * LANES] for g in range(GROUP)],
            axis=0)
        kc = jnp.concatenate([klo_ref[pr, krows, :], khi_ref[pr, krows, :]], axis=0)
        return _dot_nt(kc, qs)

    def project():
        live["qn"] = _q_heads(h, nmix_ref[...], wq_ref, bd_ref, qnw_ref[...]).astype(BF16)
        kw = jnp.concatenate([kprev_ref[...], kcur_ref[...]], axis=0)
        vwt = jnp.concatenate([vprev_ref[...], vcur_ref[...]], axis=0).T
        ones_top = jnp.broadcast_to(jnp.where(top, 1.0, 0.0), (LANES, rows + BLOCK))
        for pr in range(N_KV // 2):
            lanes = slice(pr * LANES, (pr + 1) * LANES)
            klo_ref[pr] = jnp.where(low, kw[:, lanes], 0.0).astype(BF16)
            khi_ref[pr] = jnp.where(low, 0.0, kw[:, lanes]).astype(BF16)
            vlo_ref[pr] = jnp.concatenate([jnp.where(top, vwt[lanes, :], 0.0), ones_top],
                                          axis=0).astype(BF16)
            vhi_ref[pr] = jnp.concatenate([jnp.where(top, 0.0, vwt[lanes, :]), 1.0 - ones_top],
                                          axis=0).astype(BF16)
        live["s"] = scores(*units[0])

    def unit(n):
        qb, pr = units[n]
        s = live["s"]
        if n + 1 < len(units):
            live["s"] = scores(*units[n + 1])
        qrows = slice(qb * BLOCK, (qb + 1) * BLOCK)
        krows = slice(qb * BLOCK, (qb + 2) * BLOCK)
        table = jnp.where(first_tile, 0, 1) if qb == 0 else 1
        es, sinkterms = [], []
        for half in range(2):
            kv = 2 * pr + half
            bias = jnp.concatenate([bias_ref[table, kv * GROUP + g] for g in range(GROUP)], axis=1)
            sh = s[half * 2 * BLOCK:(half + 1) * 2 * BLOCK, :] + bias
            sink = sink_ref[kv]
            m = jnp.maximum(jnp.max(sh, axis=0, keepdims=True), sink)
            es.append(jnp.exp(sh - m).astype(BF16))
            sinkterms.append(jnp.exp(sink - m))
        vct = jnp.concatenate([vlo_ref[pr, :, krows], vhi_ref[pr, :, krows]], axis=1)
        ov = _dot(vct, jnp.concatenate(es, axis=0))
        denom = ov[LANES:, :] + jnp.where(top, sinkterms[0], sinkterms[1])
        o = (ov[:LANES, :] * (1.0 / denom)).T
        for g in range(GROUP):
            o_ref[qrows, g * KV_DIM + pr * LANES:g * KV_DIM + (pr + 1) * LANES] = (
                o[g * BLOCK:(g + 1) * BLOCK, :].astype(BF16))

    def output():
        store_result(h + _dot(o_ref[...], wo_ref[...]))

    return [project] + [functools.partial(unit, n) for n in range(len(units))] + [output]


def _pipeline_flags(n_tiles, total):
    s = pl.program_id(0)
    mixer_live = s < total
    ffn_live = s >= 1
    mixer_first = lax.rem(s, n_tiles) == 0
    ffn_first = lax.rem(s + n_tiles - 1, n_tiles) == 0
    return s, mixer_live, ffn_live, mixer_first, ffn_first


def _pool_prompt_kernel(h_ref, p_ref, nmix_ref, wpool_ref, pscale_ref,
                        nffn_ref, nple_ref, wup_ref, cw_ref, cb_ref, wdown_ref, wg_ref, wp_ref,
                        hout_ref, pstate_ref, cstate_ref,
                        pcarry_ref, ucarry_ref, fs_ref, *ubufs, rows):
    t = pl.program_id(1)
    pre_p, pre_c = PROMPT_POOL_PRE, PROMPT_CONV_PRE

    @pl.when(t == 0)
    def _():
        pcarry_ref[...] = jnp.zeros((pre_p, D_MODEL), F32)
        ucarry_ref[...] = jnp.zeros((pre_c, F2), F32)

    pos = t * rows + lax.broadcasted_iota(jnp.int32, (rows, POOL_GROUP), 0)
    inv_cnt = [1.0 / jnp.minimum(pos + 1, w).astype(F32) for w in POOL_WINDOWS]
    h, state, carry = _pool_stage(h_ref[...], nmix_ref[...], pcarry_ref[...], wpool_ref,
                                  pscale_ref[...], inv_cnt, pre_p, 1, rows)
    pstate_ref[...] = state
    pcarry_ref[...] = carry
    h = _ffn_begin_prompt(h, nffn_ref[...], ucarry_ref, list(ubufs), fs_ref, wup_ref, cw_ref, cb_ref,
                          wdown_ref, pre_c, rows)()
    cstate_ref[...] = ucarry_ref[pre_c - CONV_BUF:pre_c, :]
    hout_ref[...] = _ple(h, nple_ref[...], p_ref[...], wg_ref, wp_ref)


def _attn_prompt_kernel(h_ref, p_ref, kprev_ref, kcur_ref, vprev_ref, vcur_ref,
                        nmix_ref, wq_ref, bd_ref, qnw_ref, bias_ref, sink_ref, wo_ref,
                        nffn_ref, nple_ref, wup_ref, cw_ref, cb_ref, wdown_ref, wg_ref, wp_ref,
                        hout_ref, cstate_ref,
                        hmid_ref, klo_ref, khi_ref, vlo_ref, vhi_ref, o_ref, ucarry_ref, fs_ref,
                        *ubufs, rows, n_tiles, total):
    s, _, ffn_live, mixer_first, ffn_first = _pipeline_flags(n_tiles, total)
    pre_c = PROMPT_CONV_PRE

    @pl.when(s == 0)
    def _():
        hmid_ref[...] = jnp.zeros(hmid_ref.shape, F32)

    @pl.when(jnp.logical_or(ffn_first, s == 0))
    def _():
        ucarry_ref[...] = jnp.zeros((pre_c, F2), F32)

    slot = lax.rem(s, 2)
    finish = _ffn_begin_prompt(hmid_ref[1 - slot], nffn_ref[...], ucarry_ref, list(ubufs), fs_ref,
                               wup_ref, cw_ref, cb_ref, wdown_ref, pre_c, rows, live=ffn_live)

    def store_result(value):
        hmid_ref[slot] = value

    pieces = _banded_attention_pieces(
        h_ref[...], mixer_first,
        (kprev_ref, kcur_ref, vprev_ref, vcur_ref, nmix_ref, wq_ref, bd_ref, qnw_ref, bias_ref,
         sink_ref, wo_ref, klo_ref, khi_ref, vlo_ref, vhi_ref, o_ref), store_result, rows)
    n_between = -(-D_FF // FF_CHUNK) - 1
    lead = max(len(pieces) - n_between, 1)
    for piece in pieces[:lead]:
        piece()
    h = finish(pieces[lead:])
    cstate_ref[...] = ucarry_ref[pre_c - CONV_BUF:pre_c, :]
    hout_ref[...] = _ple(h, nple_ref[...], p_ref[...], wg_ref, wp_ref)


def _decode_ffn_ple(h, refs, pre_c, step, rows):
    (p_ref, cpre_ref, nffn_ref, nple_ref, wup_ref, cw_ref, cb_ref, wdown_ref, wg_ref, wp_ref,
     hout_ref, cstate_ref, ucarry_ref, ubufs) = refs
    ucarry_ref[...] = cpre_ref[...].reshape(pre_c, F2)
    h = _ffn_begin(h, nffn_ref[...], ucarry_ref, ubufs, wup_ref, cw_ref, cb_ref, wdown_ref,
                   pre_c, step, rows)()
    cstate_ref[...] = ucarry_ref[...].reshape(cstate_ref.shape)
    p = p_ref[...].reshape(rows, PLE_DIM)
    hout_ref[...] = _ple(h, nple_ref[...], p, wg_ref, wp_ref).reshape(hout_ref.shape)


def _pool_decode_kernel(h_ref, p_ref, ppre_ref, cpre_ref, nmix_ref, wpool_ref, pscale_ref,
                        nffn_ref, nple_ref, wup_ref, cw_ref, cb_ref, wdown_ref, wg_ref, wp_ref,
                        hout_ref, pstate_ref, cstate_ref, ucarry_ref, *ubufs, rows, step):
    pre_p, pre_c = POOL_BUF * step, CONV_BUF * step
    inv_cnt = [1.0 / min(PAST_LEN + 1, w) for w in POOL_WINDOWS]
    h, state, _ = _pool_stage(h_ref[...].reshape(rows, D_MODEL), nmix_ref[...],
                              ppre_ref[...].reshape(pre_p, D_MODEL), wpool_ref, pscale_ref[...],
                              inv_cnt, pre_p, step, rows)
    pstate_ref[...] = state.reshape(pstate_ref.shape)
    _decode_ffn_ple(h, (p_ref, cpre_ref, nffn_ref, nple_ref, wup_ref, cw_ref, cb_ref, wdown_ref,
                        wg_ref, wp_ref, hout_ref, cstate_ref, ucarry_ref, list(ubufs)),
                    pre_c, step, rows)


def _attn_post_decode_kernel(h_ref, o_ref, p_ref, cpre_ref, wo_ref,
                             nffn_ref, nple_ref, wup_ref, cw_ref, cb_ref, wdown_ref, wg_ref, wp_ref,
                             hout_ref, cstate_ref, ucarry_ref, *ubufs, rows, step):
    h = h_ref[...].reshape(rows, D_MODEL) + _dot(o_ref[...].reshape(rows, D_MODEL).astype(BF16),
                                                 wo_ref[...])
    _decode_ffn_ple(h, (p_ref, cpre_ref, nffn_ref, nple_ref, wup_ref, cw_ref, cb_ref, wdown_ref,
                        wg_ref, wp_ref, hout_ref, cstate_ref, ucarry_ref, list(ubufs)),
                    CONV_BUF * step, step, rows)


def _kv_project(h, norm_ref, wk_ref, wv_ref, bd_ref, knw_ref, k_ref, v_ref):
    xn = _rmsnorm(h, norm_ref[...]).astype(BF16)
    k = _dot(xn, wk_ref[...])
    ssq = _dot((k * k).astype(BF16), bd_ref[...])
    k_ref[...] = k * lax.rsqrt(ssq * (1.0 / HEAD_DIM) + EPS) * knw_ref[...]
    v_ref[...] = _dot(xn, wv_ref[...])


def _kv_kernel(h_ref, *refs):
    _kv_project(h_ref[...], *refs)


def _softmax_sink(s, sink):
    m = jnp.maximum(jnp.max(s, axis=-1, keepdims=True), sink)
    e = jnp.exp(s - m)
    denom = jnp.sum(e, axis=-1, keepdims=True) + jnp.exp(sink - m)
    return e * (1.0 / denom)


def _decode_attn_kernel(h_ref, knew_ref, vnew_ref, ck_ref, cv_ref,
                        nmix_ref, wq_ref, bd_ref, qnw_ref, bias_ref, sink_ref, kvmask_ref, rowa_ref,
                        o_ref, qn_ref, kxa_ref, kxb_ref, vxa_ref, vxb_ref, *, kpad):
    c = pl.program_id(0)

    @pl.when(c == 0)
    def _():
        qn_ref[...] = _q_heads(h_ref[...], nmix_ref[...], wq_ref, bd_ref, qnw_ref[...])
        zeros = jnp.zeros((kpad - WINDOW - PAIR_ROWS, KV_DIM), F32)
        for ref in (kxa_ref, kxb_ref, vxa_ref, vxb_ref):
            ref[WINDOW + PAIR_ROWS:kpad, :] = zeros

    kvmask = kvmask_ref[...]
    row_a = rowa_ref[...]
    row_b = 1.0 - row_a
    bias = bias_ref[...]
    sink = sink_ref[...]

    def rows_of(i):
        return pl.ds(pl.multiple_of((c * (SEQ_CHUNK // 2) + i) * PAIR_ROWS, PAIR_ROWS), PAIR_ROWS)

    def extended(ref, cache, new, seq):
        ref[0:WINDOW, :] = cache[seq]
        ref[WINDOW:WINDOW + PAIR_ROWS, :] = new
        return ref[...].astype(BF16)

    def scores(i):
        q8 = qn_ref[rows_of(i), :]
        qg = jnp.concatenate([q8[:, g * KV_DIM:(g + 1) * KV_DIM] for g in range(GROUP)], axis=0)
        qb = (jnp.concatenate([qg] * N_KV, axis=0) * kvmask).astype(BF16)
        knew = knew_ref[rows_of(i), :]
        sa = _dot_nt(qb, extended(kxa_ref, ck_ref, knew, 2 * i))
        sb = _dot_nt(qb, extended(kxb_ref, ck_ref, knew, 2 * i + 1))
        return sa * row_a + sb * row_b + bias

    nxt = scores(0)
    for i in range(SEQ_CHUNK // 2):
        s = nxt
        if i + 1 < SEQ_CHUNK // 2:
            nxt = scores(i + 1)
        pr = _softmax_sink(s, sink)
        vnew = vnew_ref[rows_of(i), :]
        o = (_dot((pr * row_a).astype(BF16), extended(vxa_ref, cv_ref, vnew, 2 * i))
             + _dot((pr * row_b).astype(BF16), extended(vxb_ref, cv_ref, vnew, 2 * i + 1))) * kvmask
        blk = GROUP * PAIR_ROWS
        og = o[0:blk] + o[blk:2 * blk] + o[2 * blk:3 * blk] + o[3 * blk:4 * blk]
        for g in range(GROUP):
            o_ref[rows_of(i), g * KV_DIM:(g + 1) * KV_DIM] = og[g * PAIR_ROWS:(g + 1) * PAIR_ROWS]


def _resident(shape, layer=None):
    zeros = (0,) * len(shape)
    if layer is None:
        return pl.BlockSpec(shape, lambda *_: zeros, pipeline_mode=pl.Buffered(1))
    return pl.BlockSpec((None,) + shape, lambda *_: (layer,) + zeros, pipeline_mode=pl.Buffered(1))


def _params(n_axes):
    return pltpu.CompilerParams(dimension_semantics=("arbitrary",) * n_axes,
                                vmem_limit_bytes=VMEM_LIMIT_BYTES)


def _ffn_weight_specs(i):
    return [_resident((1, D_MODEL), i), _resident((1, D_MODEL), i), _resident((D_MODEL, F2), i),
            _resident((CONV_W, F2), i), _resident((1, F2), i), _resident((D_FF, D_MODEL), i),
            _resident((D_MODEL, D_MODEL), i), _resident((PLE_DIM, D_MODEL), i)]


def _ffn_weights(W):
    return [W["norm_ffn"], W["norm_ple"], W["w_up"], W["conv_w"], W["conv_b"], W["w_down"],
            W["w_ple_gate"], W["w_ple_proj"]]


def _ffn_scratch_prompt(pre_c, rows):
    slabs = pltpu.VMEM((2 * FF_CHUNK // LANES, pre_c + rows, LANES), F32)
    return ([pltpu.VMEM((pre_c, F2), F32), pltpu.VMEM((D_MODEL // LANES, rows, LANES), F32)]
            + [slabs for _ in range(UP_AHEAD + 1)])


def _ffn_scratch(pre_c, rows):
    return ([pltpu.VMEM((pre_c, F2), F32)]
            + [pltpu.VMEM((pre_c + rows, 2 * FF_CHUNK), F32) for _ in range(UP_AHEAD + 1)])


def _prompt_specs(rows, n_tiles, total):
    mixer = lambda s: jnp.minimum(s, total - 1)
    ffn = lambda s: jnp.maximum(s - 1, 0)

    def make(which, per_sequence):
        def build(r, width, layer=None):
            lead = () if layer is None else (layer,)
            index = lambda s: lead + (which(s) // n_tiles, 0 if per_sequence else which(s) % n_tiles, 0)
            return pl.BlockSpec((None,) * (len(lead) + 1) + (r, width), index)
        return build

    return make(mixer, False), make(ffn, False), make(mixer, True), make(ffn, True)


def _kv_weight_specs():
    return [_resident((1, D_MODEL)), _resident((D_MODEL, KV_DIM)), _resident((D_MODEL, KV_DIM)),
            _resident((KV_DIM, KV_DIM)), _resident((1, KV_DIM))]


def _kv_weights(W, A):
    return [W["kv_norm"], A["w_k"], A["w_v"], A["bd_k"], A["knw"]]


def _pool_prompt_layer(h, p, W, i, *, rows):
    batch, length, _ = h.shape
    assert length % rows == 0
    tile = lambda width: pl.BlockSpec((None, rows, width), lambda b, t: (b, t, 0))
    state = lambda r, width: pl.BlockSpec((None, r, width), lambda b, t: (b, 0, 0))
    in_specs = [tile(D_MODEL), pl.BlockSpec((None, None, rows, PLE_DIM), lambda b, t: (i, b, t, 0)),
                _resident((1, D_MODEL), i),
                _resident((len(POOL_WINDOWS), POOL_GROUP, POOL_GROUP), i),
                _resident((1, D_MODEL), i)] + _ffn_weight_specs(i)
    args = [h, p, W["norm_mix"], W["w_pool"], W["pool_scale"]] + _ffn_weights(W)
    kern = functools.partial(_pool_prompt_kernel, rows=rows)
    return pl.pallas_call(
        kern,
        grid=(batch, length // rows),
        in_specs=in_specs,
        out_specs=[tile(D_MODEL), state(POOL_BUF, D_MODEL), state(CONV_BUF, F2)],
        out_shape=[jax.ShapeDtypeStruct(h.shape, F32),
                   jax.ShapeDtypeStruct((batch, POOL_BUF, D_MODEL), F32),
                   jax.ShapeDtypeStruct((batch, CONV_BUF, F2), F32)],
        scratch_shapes=[pltpu.VMEM((PROMPT_POOL_PRE, D_MODEL), F32)]
        + _ffn_scratch_prompt(PROMPT_CONV_PRE, rows),
        compiler_params=_params(2),
        name="pool_layer_prompt",
    )(*args)


def _attn_prompt_layer(h, p, k, v, W, A, i, *, rows):
    batch, length, _ = h.shape
    j = i - N_A
    n_tiles = length // rows
    total = batch * n_tiles
    blocks = rows // BLOCK
    assert n_tiles > 1 and length % rows == 0
    mixer_tile, ffn_tile, _, ffn_state = _prompt_specs(rows, n_tiles, total)
    mixer = lambda s: jnp.minimum(s, total - 1)
    prev = pl.BlockSpec(
        (None, BLOCK, KV_DIM),
        lambda s: (mixer(s) // n_tiles, jnp.maximum((mixer(s) % n_tiles) * blocks - 1, 0), 0))
    in_specs = [mixer_tile(rows, D_MODEL), ffn_tile(rows, PLE_DIM, i),
                prev, mixer_tile(rows, KV_DIM), prev, mixer_tile(rows, KV_DIM),
                _resident((1, D_MODEL), i), _resident((D_MODEL, D_MODEL), j),
                _resident((D_MODEL, D_MODEL)), _resident((1, D_MODEL), j),
                _resident((2, N_HEADS, 2 * BLOCK, BLOCK)),
                _resident((N_KV, 1, GROUP * BLOCK), j), _resident((D_MODEL, D_MODEL), j)]
    in_specs += _ffn_weight_specs(i)
    args = [h, p, k, k, v, v, W["norm_mix"], A["w_q"], A["bd_q"], A["qnw"],
            A["bias_prompt"], A["sink_prompt"], A["w_o"]] + _ffn_weights(W)
    kern = functools.partial(_attn_prompt_kernel, rows=rows, n_tiles=n_tiles, total=total)
    khalf = pltpu.VMEM((N_KV // 2, rows + BLOCK, LANES), BF16)
    vhalf = pltpu.VMEM((N_KV // 2, 2 * LANES, rows + BLOCK), BF16)
    return pl.pallas_call(
        kern,
        grid=(total + 1,),
        in_specs=in_specs,
        out_specs=[ffn_tile(rows, D_MODEL), ffn_state(CONV_BUF, F2)],
        out_shape=[jax.ShapeDtypeStruct(h.shape, F32),
                   jax.ShapeDtypeStruct((batch, CONV_BUF, F2), F32)],
        scratch_shapes=[pltpu.VMEM((2, rows, D_MODEL), F32), khalf, khalf, vhalf, vhalf,
                        pltpu.VMEM((rows, D_MODEL), BF16)] + _ffn_scratch_prompt(PROMPT_CONV_PRE, rows),
        compiler_params=_params(1),
        name="attn_layer_prompt",
    )(*args)


def _decode_specs(rows, step):
    def build(r, width, layer=None):
        lead = () if layer is None else (layer,)
        return pl.BlockSpec((None,) * len(lead) + (r, step, width), lambda t: lead + (0, t, 0))
    return build


def _pool_decode_layer(h, p, ppre, cpre, W, i, *, rows, step):
    steps, n_seq, _ = h.shape
    block = _decode_specs(rows, step)
    in_specs = [block(steps, D_MODEL), block(steps, PLE_DIM, i), block(POOL_BUF, D_MODEL, i),
                block(CONV_BUF, F2, i), _resident((1, D_MODEL), i),
                _resident((len(POOL_WINDOWS), POOL_GROUP, POOL_GROUP), i),
                _resident((1, D_MODEL), i)] + _ffn_weight_specs(i)
    kern = functools.partial(_pool_decode_kernel, rows=rows, step=step)
    return pl.pallas_call(
        kern,
        grid=(n_seq // step,),
        in_specs=in_specs,
        out_specs=[block(steps, D_MODEL), block(POOL_BUF, D_MODEL), block(CONV_BUF, F2)],
        out_shape=[jax.ShapeDtypeStruct(h.shape, F32),
                   jax.ShapeDtypeStruct((POOL_BUF, n_seq, D_MODEL), F32),
                   jax.ShapeDtypeStruct((CONV_BUF, n_seq, F2), F32)],
        scratch_shapes=_ffn_scratch(CONV_BUF * step, rows),
        compiler_params=_params(1),
        name="pool_layer_decode",
    )(h, p, ppre, cpre, W["norm_mix"], W["w_pool"], W["pool_scale"], *_ffn_weights(W))


def _attn_post_decode_layer(h, o, p, cpre, W, A, i, *, rows, step):
    steps, n_seq, _ = h.shape
    j = i - N_A
    block = _decode_specs(rows, step)
    kern = functools.partial(_attn_post_decode_kernel, rows=rows, step=step)
    return pl.pallas_call(
        kern,
        grid=(n_seq // step,),
        in_specs=[block(steps, D_MODEL), block(steps, D_MODEL), block(steps, PLE_DIM, i),
                  block(CONV_BUF, F2, i), _resident((D_MODEL, D_MODEL), j)] + _ffn_weight_specs(i),
        out_specs=[block(steps, D_MODEL), block(CONV_BUF, F2)],
        out_shape=[jax.ShapeDtypeStruct(h.shape, F32),
                   jax.ShapeDtypeStruct((CONV_BUF, n_seq, F2), F32)],
        scratch_shapes=_ffn_scratch(CONV_BUF * step, rows),
        compiler_params=_params(1),
        name="attn_post_layer_decode",
    )(h, o, p, cpre, A["w_o"], *_ffn_weights(W))


def _shared_kv(h2d, W, A, *, rows):
    total = h2d.shape[0]
    tile = lambda width: pl.BlockSpec((rows, width), lambda t: (t, 0))
    return pl.pallas_call(
        _kv_kernel,
        grid=(total // rows,),
        in_specs=[tile(D_MODEL)] + _kv_weight_specs(),
        out_specs=[tile(KV_DIM), tile(KV_DIM)],
        out_shape=[jax.ShapeDtypeStruct((total, KV_DIM), F32)] * 2,
        compiler_params=_params(1),
        name="shared_kv",
    )(h2d, *_kv_weights(W, A))


def _decode_attn(h_bm, knew, vnew, cache_k, cache_v, W, A, i):
    rows = h_bm.shape[0]
    n_seq = cache_k.shape[0]
    j = i - N_A
    kpad = A["bias_decode"].shape[1]
    qrows = N_KV * GROUP * PAIR_ROWS
    full = lambda r, width: pl.BlockSpec((r, width), lambda c: (0, 0))
    cache = pl.BlockSpec((SEQ_CHUNK, WINDOW, KV_DIM), lambda c: (c, 0, 0))
    kern = functools.partial(_decode_attn_kernel, kpad=kpad)
    ext = pltpu.VMEM((kpad, KV_DIM), F32)
    return pl.pallas_call(
        kern,
        grid=(n_seq // SEQ_CHUNK,),
        in_specs=[full(rows, D_MODEL), full(rows, KV_DIM), full(rows, KV_DIM), cache, cache,
                  _resident((1, D_MODEL), i), _resident((D_MODEL, D_MODEL), j),
                  _resident((D_MODEL, D_MODEL)), _resident((1, D_MODEL), j),
                  _resident((qrows, kpad)), _resident((qrows, 1), j),
                  _resident((qrows, KV_DIM)), _resident((qrows, 1))],
        out_specs=full(rows, D_MODEL),
        out_shape=jax.ShapeDtypeStruct((rows, D_MODEL), F32),
        scratch_shapes=[pltpu.VMEM((rows, D_MODEL), F32), ext, ext, ext, ext],
        compiler_params=_params(1),
        name="decode_attn",
    )(h_bm, knew, vnew, cache_k, cache_v, W["norm_mix"], A["w_q"], A["bd_q"], A["qnw"],
      A["bias_decode"], A["sink_decode"], A["kvmask"], A["row_a"])


def _t5_bucket(d):
    n = jnp.maximum(d, 0)
    max_exact = N_BUCKETS // 2
    nf = jnp.maximum(n, 1).astype(F32)
    large = max_exact + (jnp.log(nf / max_exact) / math.log(MAX_DISTANCE / max_exact)
                         * (N_BUCKETS - max_exact)).astype(jnp.int32)
    large = jnp.minimum(large, N_BUCKETS - 1)
    return jnp.where(n < max_exact, n, large)


def _block_diag_ones(width):
    idx = np.arange(width) // HEAD_DIM
    return jnp.asarray(idx[:, None] == idx[None, :], BF16)


def _banded_bias(tab):
    heads = tab.shape[1]
    period = 3 * BLOCK + 1
    base = jnp.concatenate([jnp.full((heads, BLOCK - 1), NEG_INF, F32), tab.T,
                            jnp.full((heads, period - 2 * BLOCK + 1), NEG_INF, F32)], axis=1)
    flat = jnp.tile(base, (1, 2 * BLOCK))[:, :2 * BLOCK * (period - 1)]
    skew = flat.reshape(heads, 2 * BLOCK, period - 1)
    return skew[:, :, 2 * BLOCK - 1:3 * BLOCK - 1]


def _attention_tables(rel_bias, sinks):
    A = {}
    tab = rel_bias[_t5_bucket(jnp.arange(WINDOW))].astype(F32)

    steady = _banded_bias(tab)
    has_prev = (np.arange(2 * BLOCK) >= BLOCK)[None, :, None]
    A["bias_prompt"] = jnp.stack([jnp.where(has_prev, steady, NEG_INF), steady])

    tab3 = tab.reshape(WINDOW, N_KV, GROUP)
    steps = PAIR_ROWS // 2
    kpad = -(-(WINDOW + PAIR_ROWS) // 16) * 16
    col = np.arange(kpad)[None, :]
    r = np.arange(PAIR_ROWS)[:, None]
    step_of = r % steps
    seq_b = r >= steps
    keyidx = np.where(col < WINDOW, col, WINDOW + (col - WINDOW) % steps)
    own = (col < WINDOW) | ((col < WINDOW + PAIR_ROWS) & (((col - WINDOW) >= steps) == seq_b))
    dd = step_of + WINDOW - keyidx
    dvalid = own & (dd >= 0) & (dd < WINDOW)
    ddc = np.clip(dd, 0, WINDOW - 1)
    dec = jnp.where(dvalid[:, :, None, None], tab3[ddc], NEG_INF)
    A["bias_decode"] = dec.transpose(2, 3, 0, 1).reshape(N_KV * GROUP * PAIR_ROWS, kpad)
    A["row_a"] = jnp.asarray(np.tile(~seq_b, (N_KV * GROUP, 1)).reshape(-1, 1), F32)
    lane_kv = np.arange(KV_DIM)[None, :] // HEAD_DIM
    row_kv = np.arange(N_KV * GROUP * PAIR_ROWS)[:, None] // (GROUP * PAIR_ROWS)
    A["kvmask"] = jnp.asarray(lane_kv == row_kv, F32)

    sk = sinks.astype(F32).reshape(-1, N_KV, GROUP)
    A["sink_prompt"] = jnp.repeat(sk, BLOCK, axis=2)[:, :, None, :]
    A["sink_decode"] = jnp.repeat(sk.reshape(sk.shape[0], -1), PAIR_ROWS, axis=1)[..., None]
    return A


def _prepare(W):
    A = _attention_tables(W["rel_bias"], W["sinks"])
    n_b = W["w_q"].shape[0]
    A["w_q"] = (W["w_q"].reshape(n_b, D_MODEL, N_KV, GROUP, HEAD_DIM).transpose(0, 1, 3, 2, 4)
                .reshape(n_b, D_MODEL, D_MODEL).astype(BF16))
    A["w_o"] = (W["w_o"].reshape(n_b, N_KV, GROUP, HEAD_DIM, D_MODEL).transpose(0, 2, 1, 3, 4)
                .reshape(n_b, D_MODEL, D_MODEL).astype(BF16))
    A["qnw"] = jnp.tile(W["q_norm"], (1, N_HEADS)).reshape(n_b, 1, D_MODEL)
    A["knw"] = jnp.tile(W["k_norm"], N_KV).reshape(1, KV_DIM)
    A["w_k"] = W["w_k"].astype(BF16)
    A["w_v"] = W["w_v"].astype(BF16)
    A["bd_q"] = _block_diag_ones(D_MODEL)
    A["bd_k"] = _block_diag_ones(KV_DIM)
    V = dict(W)
    for name in ("w_pool", "w_up", "w_down", "w_ple_gate", "w_ple_proj"):
        V[name] = W[name].astype(BF16)
    for name in ("norm_mix", "norm_ffn", "norm_ple", "pool_scale", "conv_b"):
        V[name] = W[name][:, None, :]
    V["kv_norm"] = W["kv_norm"][None, :]
    return V, A


def kernel(x_prompt, x_sample, p_prompt, p_sample, state_pool, state_conv, cache_k, cache_v,
           norm_mix, norm_ffn, norm_ple, w_pool, pool_scale, kv_norm, w_k, w_v, k_norm,
           w_q, q_norm, sinks, w_o, rel_bias, w_up, conv_w, conv_b, w_down, w_ple_gate, w_ple_proj):
    W, A = _prepare(dict(
        norm_mix=norm_mix, norm_ffn=norm_ffn, norm_ple=norm_ple, w_pool=w_pool,
        pool_scale=pool_scale, kv_norm=kv_norm, w_k=w_k, w_v=w_v, k_norm=k_norm, w_q=w_q,
        q_norm=q_norm, sinks=sinks, w_o=w_o, rel_bias=rel_bias, w_up=w_up, conv_w=conv_w,
        conv_b=conv_b, w_down=w_down, w_ple_gate=w_ple_gate, w_ple_proj=w_ple_proj))
    batch, seq, _ = x_prompt.shape
    n_seq, steps, _ = x_sample.shape

    h = x_prompt
    pool_p, conv_p = [], []
    for i in range(N_A):
        h, ps, cs = _pool_prompt_layer(h, p_prompt, W, i, rows=PROMPT_TILE)
        pool_p.append(ps)
        conv_p.append(cs)
    k_p, v_p = _shared_kv(h.reshape(batch * seq, D_MODEL), W, A, rows=KV_TILE)
    k_p = k_p.reshape(batch, seq, KV_DIM)
    v_p = v_p.reshape(batch, seq, KV_DIM)
    for i in range(N_A, DEPTH):
        h, cs = _attn_prompt_layer(h, p_prompt, k_p, v_p, W, A, i, rows=PROMPT_TILE)
        conv_p.append(cs)
    y_prompt = h

    swap = lambda x: jnp.swapaxes(x, -3, -2)
    tile_rows = steps * DECODE_SEQ_TILE
    total = steps * n_seq
    hs = swap(x_sample)
    ps_sw, ppre_sw, cpre_sw = swap(p_sample), swap(state_pool), swap(state_conv)
    pool_s, conv_s = [], []
    for i in range(N_A):
        hs, ps, cs = _pool_decode_layer(hs, ps_sw, ppre_sw, cpre_sw, W, i, rows=tile_rows,
                                        step=DECODE_SEQ_TILE)
        pool_s.append(ps)
        conv_s.append(cs)
    hs_bm = swap(hs).reshape(total, D_MODEL)
    k_new, v_new = _shared_kv(hs_bm, W, A, rows=total)
    ck = cache_k.reshape(n_seq, WINDOW, KV_DIM)
    cv = cache_v.reshape(n_seq, WINDOW, KV_DIM)
    for i in range(N_A, DEPTH):
        o_bm = _decode_attn(hs_bm, k_new, v_new, ck, cv, W, A, i)
        hs, cs = _attn_post_decode_layer(hs, swap(o_bm.reshape(n_seq, steps, D_MODEL)), ps_sw,
                                         cpre_sw, W, A, i, rows=tile_rows, step=DECODE_SEQ_TILE)
        conv_s.append(cs)
        hs_bm = swap(hs).reshape(total, D_MODEL)
    y_sample = hs_bm.reshape(n_seq, steps, D_MODEL)

    def window(cache, new):
        ext = jnp.concatenate([cache, new.reshape(n_seq, steps, N_KV, HEAD_DIM)], axis=1)
        return ext[:, -WINDOW:]

    kp4 = k_p[:, -WINDOW:].reshape(batch, WINDOW, N_KV, HEAD_DIM)
    vp4 = v_p[:, -WINDOW:].reshape(batch, WINDOW, N_KV, HEAD_DIM)
    return (y_prompt, y_sample, jnp.stack(pool_p), swap(jnp.stack(pool_s)), jnp.stack(conv_p),
            swap(jnp.stack(conv_s)), kp4, window(cache_k, k_new), vp4, window(cache_v, v_new))
```

```python
import functools
import math

import numpy as np
import jax
import jax.numpy as jnp
from jax import lax
from jax.experimental import pallas as pl
from jax.experimental.pallas import tpu as pltpu

D_MODEL = 1024
DEPTH = 4
N_A = DEPTH // 2
POOL_WINDOWS = (2, 4, 8, 16)
POOL_GROUP = D_MODEL // len(POOL_WINDOWS)
POOL_BUF = max(POOL_WINDOWS) - 1
HEAD_DIM = 64
N_HEADS = D_MODEL // HEAD_DIM
N_KV = 4
GROUP = N_HEADS // N_KV
KV_DIM = N_KV * HEAD_DIM
WINDOW = 128
BLOCK = WINDOW
N_BUCKETS = 32
MAX_DISTANCE = 128
D_FF = 2816
F2 = 2 * D_FF
CONV_W = 3
CONV_BUF = CONV_W - 1
PLE_DIM = 256
EPS = 1e-6
PAST_LEN = 8192

LANES = 128
SUBLANES = 8
VMEM_LIMIT_BYTES = 58 * 1024 * 1024

PROMPT_TILE = 256
KV_TILE = 1024
FF_CHUNK = 512
UP_AHEAD = 2
ROW_PHASES = 4
DECODE_SEQ_TILE = 64
SEQ_CHUNK = 16
PAIR_ROWS = 2 * 4
PROMPT_POOL_PRE = 2 * SUBLANES
PROMPT_CONV_PRE = SUBLANES

F32 = jnp.float32
BF16 = jnp.bfloat16
NEG_INF = float("-inf")


def _rmsnorm(x, g):
    ms = jnp.mean(x * x, axis=-1, keepdims=True)
    return x * lax.rsqrt(ms + EPS) * g


def _gelu_tanh(x):
    c = np.float32(math.sqrt(2.0 / math.pi))
    cdf = 0.5 * (1.0 + jnp.tanh(c * (x + 0.044715 * (x * x * x))))
    return x * cdf


def _sigmoid(x):
    return 1.0 / (1.0 + jnp.exp(-x))


def _dot(a, b):
    return jnp.dot(a, b, preferred_element_type=F32)


def _dot_nt(a, b):
    return lax.dot_general(a, b, (((1,), (1,)), ((), ())), preferred_element_type=F32)


def _pool_stage(h, norm_g, carried, wpool_ref, scale, inv_cnt, pre, step, rows):
    xn = _rmsnorm(h, norm_g)
    ext = jnp.concatenate([carried, xn], axis=0)
    state = ext[pre + rows - POOL_BUF * step:pre + rows, :]
    carry = ext[rows:rows + pre, :]
    outs = []
    for g, w in enumerate(POOL_WINDOWS):
        cols = slice(g * POOL_GROUP, (g + 1) * POOL_GROUP)
        s = ext[:, cols]
        width = 1
        while width < w:
            s = s + pltpu.roll(s, width * step, 0)
            width *= 2
        d = s[pre:pre + rows, :] * inv_cnt[g] - xn[:, cols]
        outs.append(_dot(d.astype(BF16), wpool_ref[g]))
    return h + jnp.concatenate(outs, axis=1) * scale, state, carry


def _ffn_begin(h, norm_g, ucarry_ref, ubufs, wup_ref, cw_ref, cb_ref, wdown_ref, pre, step, rows):
    xn = _rmsnorm(h, norm_g).astype(BF16)
    bounds = list(range(0, D_FF, FF_CHUNK)) + [D_FF]
    n_chunks = len(bounds) - 1
    halves = lambda c: [slice(base + bounds[c], base + bounds[c + 1]) for base in (0, D_FF)]
    local = lambda c: [slice(j * FF_CHUNK, j * FF_CHUNK + bounds[c + 1] - bounds[c]) for j in range(2)]

    def up(c):
        ub = ubufs[c % len(ubufs)]
        for cols, loc in zip(halves(c), local(c)):
            ub[0:pre, loc] = ucarry_ref[:, cols]
            ub[pre:pre + rows, loc] = _dot(xn, wup_ref[:, cols])

    for c in range(min(UP_AHEAD, n_chunks)):
        up(c)

    def finish(live=None, between=()):
        f = None
        for c in range(n_chunks):
            if c + UP_AHEAD < n_chunks:
                up(c + UP_AHEAD)
            ub = ubufs[c % len(ubufs)]
            conv = []
            for cols, loc in zip(halves(c), local(c)):
                ue = ub[:, loc]
                tail = ue[rows:rows + pre, :]
                ucarry_ref[:, cols] = tail if live is None else jnp.where(live, tail, ue[0:pre, :])
                acc = cb_ref[:, cols] + pltpu.roll(ue, 2 * step, 0)[pre:pre + rows, :] * cw_ref[0:1, cols]
                acc = acc + pltpu.roll(ue, step, 0)[pre:pre + rows, :] * cw_ref[1:2, cols]
                conv.append(acc + ue[pre:pre + rows, :] * cw_ref[2:3, cols])
            a = (_gelu_tanh(conv[0]) * conv[1]).astype(BF16)
            part = _dot(a, wdown_ref[bounds[c]:bounds[c + 1], :])
            f = part if f is None else f + part
            if c < len(between):
                between[c]()
        return h + f

    return finish


def _ffn_begin_prompt(h, norm_g, ucarry_ref, ubufs, fs_ref, wup_ref, cw_ref, cb_ref, wdown_ref,
                      pre, rows, live=None):
    xn = _rmsnorm(h, norm_g).astype(BF16)
    bounds = list(range(0, D_FF, FF_CHUNK)) + [D_FF]
    n_chunks = len(bounds) - 1
    slabs_per_half = FF_CHUNK // LANES
    part_rows = rows // ROW_PHASES

    def up(c):
        ub = ubufs[c % len(ubufs)]
        for half, base in enumerate((0, D_FF)):
            cols = slice(base + bounds[c], base + bounds[c + 1])
            u = _dot(xn, wup_ref[:, cols])
            old = ucarry_ref[:, cols]
            tail = u[rows - pre:rows, :]
            ucarry_ref[:, cols] = tail if live is None else jnp.where(live, tail, old)
            for k in range((bounds[c + 1] - bounds[c]) // LANES):
                lanes = slice(k * LANES, (k + 1) * LANES)
                ub[half * slabs_per_half + k, 0:pre, :] = old[:, lanes]
                ub[half * slabs_per_half + k, pre:pre + rows, :] = u[:, lanes]

    for c in range(min(UP_AHEAD, n_chunks)):
        up(c)

    def finish(between=()):
        f = None
        for c in range(n_chunks):
            ub = ubufs[c % len(ubufs)]
            phases = []
            for s in range(ROW_PHASES):
                slabs = []
                for k in range((bounds[c + 1] - bounds[c]) // LANES):
                    conv = []
                    for half, base in enumerate((0, D_FF)):
                        cols = slice(base + bounds[c] + k * LANES, base + bounds[c] + (k + 1) * LANES)
                        slab = half * slabs_per_half + k
                        tap = lambda d: ub[slab, pl.ds(pre + s - d, part_rows, stride=ROW_PHASES), :]
                        acc = cb_ref[:, cols] + tap(2) * cw_ref[0:1, cols]
                        acc = acc + tap(1) * cw_ref[1:2, cols]
                        conv.append(acc + tap(0) * cw_ref[2:3, cols])
                    slabs.append((_gelu_tanh(conv[0]) * conv[1]).astype(BF16))
                phases.append(jnp.concatenate(slabs, axis=1))
            a = jnp.concatenate(phases, axis=0)
            if c + UP_AHEAD < n_chunks:
                up(c + UP_AHEAD)
            part = _dot(a, wdown_ref[bounds[c]:bounds[c + 1], :])
            f = part if f is None else f + part
            if c < len(between):
                between[c]()
        for s in range(ROW_PHASES):
            for k in range(D_MODEL // LANES):
                fs_ref[k, pl.ds(s, part_rows, stride=ROW_PHASES), :] = (
                    f[s * part_rows:(s + 1) * part_rows, k * LANES:(k + 1) * LANES])
        return h + jnp.concatenate([fs_ref[k] for k in range(D_MODEL // LANES)], axis=1)

    return finish


def _ple(h, norm_g, p, wg_ref, wp_ref):
    xn = _rmsnorm(h, norm_g).astype(BF16)
    gate = _sigmoid(_dot(xn, wg_ref[...]))
    return h + gate * _dot(p.astype(BF16), wp_ref[...])


def _q_heads(h, norm_g, wq_ref, bd_ref, qnw):
    xn = _rmsnorm(h, norm_g).astype(BF16)
    q = _dot(xn, wq_ref[...])
    ssq = _dot((q * q).astype(BF16), bd_ref[...])
    return q * lax.rsqrt(ssq * (1.0 / HEAD_DIM) + EPS) * (qnw * HEAD_DIM ** -0.5)


def _banded_attention_pieces(h, first_tile, kv_window, refs, store_result, rows):
    (nmix_ref, wq_ref, he_ref, qnw_ref, bias_ref,
     sink_ref, wo_ref, klo_ref, khi_ref, vlo_ref, vhi_ref, o_ref) = refs
    low = lax.broadcasted_iota(jnp.int32, (1, LANES), 1) < HEAD_DIM
    top = lax.broadcasted_iota(jnp.int32, (LANES, 1), 0) < HEAD_DIM
    units = [(qb, pr) for qb in range(rows // BLOCK) for pr in range(N_KV // 2)]
    live = {}

    def scores(qb, pr):
        qrows = slice(qb * BLOCK, (qb + 1) * BLOCK)
        krows = slice(qb * BLOCK, (qb + 2) * BLOCK)
        qn = live["qn"]
        qs = jnp.concatenate(
            [qn[qrows, g * KV_DIM + pr * LANES:g * KV_DIM + (pr + 1) * LANES] for g in range(GROUP)],
            axis=0)
        kc = jnp.concatenate([klo_ref[pr, krows, :], khi_ref[pr, krows, :]], axis=0)
        return _dot_nt(kc, qs)

    def project():
        xn = _rmsnorm(h, nmix_ref[...]).astype(BF16)
        q = _dot(xn, wq_ref[...])
        ssq = _dot((q * q).astype(BF16), he_ref[...])
        live["rinv_t"] = lax.rsqrt(ssq * (1.0 / HEAD_DIM) + EPS).T
        live["qn"] = (q * HEAD_DIM ** -0.5).astype(BF16)
        kw, vw = kv_window()
        kw = kw * qnw_ref[:, 0:KV_DIM]
        vwt = vw.T
        ones_top = jnp.broadcast_to(jnp.where(top, 1.0, 0.0), (LANES, rows + BLOCK))
        for pr in range(N_KV // 2):
            lanes = slice(pr * LANES, (pr + 1) * LANES)
            klo_ref[pr] = jnp.where(low, kw[:, lanes], 0.0).astype(BF16)
            khi_ref[pr] = jnp.where(low, 0.0, kw[:, lanes]).astype(BF16)
            vlo_ref[pr] = jnp.concatenate([jnp.where(top, vwt[lanes, :], 0.0), ones_top],
                                          axis=0).astype(BF16)
            vhi_ref[pr] = jnp.concatenate([jnp.where(top, 0.0, vwt[lanes, :]), 1.0 - ones_top],
                                          axis=0).astype(BF16)
        live["s"] = scores(*units[0])

    def unit(n):
        qb, pr = units[n]
        s = live["s"]
        if n + 1 < len(units):
            live["s"] = scores(*units[n + 1])
        qrows = slice(qb * BLOCK, (qb + 1) * BLOCK)
        krows = slice(qb * BLOCK, (qb + 2) * BLOCK)
        table = jnp.where(first_tile, 0, 1) if qb == 0 else 1
        es, sinkterms = [], []
        for half in range(2):
            kv = 2 * pr + half
            bias = jnp.concatenate([bias_ref[table, kv * GROUP + g] for g in range(GROUP)], axis=1)
            rinv = jnp.concatenate(
                [live["rinv_t"][g * N_KV + kv:g * N_KV + kv + 1, qrows] for g in range(GROUP)], axis=1)
            sh = s[half * 2 * BLOCK:(half + 1) * 2 * BLOCK, :] * rinv + bias
            sink = sink_ref[kv]
            m = jnp.maximum(jnp.max(sh, axis=0, keepdims=True), sink)
            es.append(jnp.exp(sh - m).astype(BF16))
            sinkterms.append(jnp.exp(sink - m))
        vct = jnp.concatenate([vlo_ref[pr, :, krows], vhi_ref[pr, :, krows]], axis=1)
        ov = _dot(vct, jnp.concatenate(es, axis=0))
        denom = ov[LANES:, :] + jnp.where(top, sinkterms[0], sinkterms[1])
        o = (ov[:LANES, :] * (1.0 / denom)).T
        for g in range(GROUP):
            o_ref[qrows, g * KV_DIM + pr * LANES:g * KV_DIM + (pr + 1) * LANES] = (
                o[g * BLOCK:(g + 1) * BLOCK, :].astype(BF16))

    def output():
        store_result(h + _dot(o_ref[...], wo_ref[...]))

    return [project] + [functools.partial(unit, n) for n in range(len(units))] + [output]


def _pipeline_flags(n_tiles, total):
    s = pl.program_id(0)
    mixer_live = s < total
    ffn_live = s >= 1
    mixer_first = lax.rem(s, n_tiles) == 0
    ffn_first = lax.rem(s + n_tiles - 1, n_tiles) == 0
    return s, mixer_live, ffn_live, mixer_first, ffn_first


def _pool_prompt_kernel(h_ref, p_ref, nmix_ref, wpool_ref, pscale_ref,
                        nffn_ref, nple_ref, wup_ref, cw_ref, cb_ref, wdown_ref, wg_ref, wp_ref,
                        hout_ref, pstate_ref, cstate_ref,
                        pcarry_ref, ucarry_ref, fs_ref, *ubufs, rows):
    t = pl.program_id(1)
    pre_p, pre_c = PROMPT_POOL_PRE, PROMPT_CONV_PRE

    @pl.when(t == 0)
    def _():
        pcarry_ref[...] = jnp.zeros((pre_p, D_MODEL), F32)
        ucarry_ref[...] = jnp.zeros((pre_c, F2), F32)

    pos = t * rows + lax.broadcasted_iota(jnp.int32, (rows, POOL_GROUP), 0)
    inv_cnt = [1.0 / jnp.minimum(pos + 1, w).astype(F32) for w in POOL_WINDOWS]
    h, state, carry = _pool_stage(h_ref[...], nmix_ref[...], pcarry_ref[...], wpool_ref,
                                  pscale_ref[...], inv_cnt, pre_p, 1, rows)
    pstate_ref[...] = state
    pcarry_ref[...] = carry
    h = _ffn_begin_prompt(h, nffn_ref[...], ucarry_ref, list(ubufs), fs_ref, wup_ref, cw_ref, cb_ref,
                          wdown_ref, pre_c, rows)()
    cstate_ref[...] = ucarry_ref[pre_c - CONV_BUF:pre_c, :]
    hout_ref[...] = _ple(h, nple_ref[...], p_ref[...], wg_ref, wp_ref)


def _attn_prompt_kernel(h_ref, p_ref, kprev_ref, kcur_ref, vprev_ref, vcur_ref,
                        nmix_ref, wq_ref, he_ref, qnw_ref, bias_ref, sink_ref, wo_ref,
                        nffn_ref, nple_ref, wup_ref, cw_ref, cb_ref, wdown_ref, wg_ref, wp_ref,
                        hout_ref, cstate_ref,
                        hmid_ref, klo_ref, khi_ref, vlo_ref, vhi_ref, o_ref, ucarry_ref, fs_ref,
                        *ubufs, rows, n_tiles, total):
    s, _, ffn_live, mixer_first, ffn_first = _pipeline_flags(n_tiles, total)
    pre_c = PROMPT_CONV_PRE

    @pl.when(s == 0)
    def _():
        hmid_ref[...] = jnp.zeros(hmid_ref.shape, F32)

    def kv_window():
        return (jnp.concatenate([kprev_ref[...], kcur_ref[...]], axis=0),
                jnp.concatenate([vprev_ref[...], vcur_ref[...]], axis=0))

    @pl.when(jnp.logical_or(ffn_first, s == 0))
    def _():
        ucarry_ref[...] = jnp.zeros((pre_c, F2), F32)

    slot = lax.rem(s, 2)
    finish = _ffn_begin_prompt(hmid_ref[1 - slot], nffn_ref[...], ucarry_ref, list(ubufs), fs_ref,
                               wup_ref, cw_ref, cb_ref, wdown_ref, pre_c, rows, live=ffn_live)

    def store_result(value):
        hmid_ref[slot] = value

    pieces = _banded_attention_pieces(
        h_ref[...], mixer_first, kv_window,
        (nmix_ref, wq_ref, he_ref, qnw_ref, bias_ref,
         sink_ref, wo_ref, klo_ref, khi_ref, vlo_ref, vhi_ref, o_ref), store_result, rows)
    n_between = -(-D_FF // FF_CHUNK) - 1
    lead = max(len(pieces) - n_between, 1)
    for piece in pieces[:lead]:
        piece()
    h = finish(pieces[lead:])
    cstate_ref[...] = ucarry_ref[pre_c - CONV_BUF:pre_c, :]
    hout_ref[...] = _ple(h, nple_ref[...], p_ref[...], wg_ref, wp_ref)


def _decode_ffn_ple(h, refs, pre_c, step, rows):
    (p_ref, cpre_ref, nffn_ref, nple_ref, wup_ref, cw_ref, cb_ref, wdown_ref, wg_ref, wp_ref,
     hout_ref, cstate_ref, ucarry_ref, ubufs) = refs
    ucarry_ref[...] = cpre_ref[...].reshape(pre_c, F2)
    h = _ffn_begin(h, nffn_ref[...], ucarry_ref, ubufs, wup_ref, cw_ref, cb_ref, wdown_ref,
                   pre_c, step, rows)()
    cstate_ref[...] = ucarry_ref[...].reshape(cstate_ref.shape)
    p = p_ref[...].reshape(rows, PLE_DIM)
    hout_ref[...] = _ple(h, nple_ref[...], p, wg_ref, wp_ref).reshape(hout_ref.shape)


def _pool_decode_kernel(h_ref, p_ref, ppre_ref, cpre_ref, nmix_ref, wpool_ref, pscale_ref,
                        nffn_ref, nple_ref, wup_ref, cw_ref, cb_ref, wdown_ref, wg_ref, wp_ref,
                        hout_ref, pstate_ref, cstate_ref, ucarry_ref, *ubufs, rows, step):
    pre_p, pre_c = POOL_BUF * step, CONV_BUF * step
    inv_cnt = [1.0 / min(PAST_LEN + 1, w) for w in POOL_WINDOWS]
    h, state, _ = _pool_stage(h_ref[...].reshape(rows, D_MODEL), nmix_ref[...],
                              ppre_ref[...].reshape(pre_p, D_MODEL), wpool_ref, pscale_ref[...],
                              inv_cnt, pre_p, step, rows)
    pstate_ref[...] = state.reshape(pstate_ref.shape)
    _decode_ffn_ple(h, (p_ref, cpre_ref, nffn_ref, nple_ref, wup_ref, cw_ref, cb_ref, wdown_ref,
                        wg_ref, wp_ref, hout_ref, cstate_ref, ucarry_ref, list(ubufs)),
                    pre_c, step, rows)


def _attn_post_decode_kernel(h_ref, o_ref, p_ref, cpre_ref, wo_ref,
                             nffn_ref, nple_ref, wup_ref, cw_ref, cb_ref, wdown_ref, wg_ref, wp_ref,
                             hout_ref, cstate_ref, ucarry_ref, *ubufs, rows, step):
    h = h_ref[...].reshape(rows, D_MODEL) + _dot(o_ref[...].reshape(rows, D_MODEL).astype(BF16),
                                                 wo_ref[...])
    _decode_ffn_ple(h, (p_ref, cpre_ref, nffn_ref, nple_ref, wup_ref, cw_ref, cb_ref, wdown_ref,
                        wg_ref, wp_ref, hout_ref, cstate_ref, ucarry_ref, list(ubufs)),
                    CONV_BUF * step, step, rows)


def _kv_project(h, norm_ref, wk_ref, wv_ref, bd_ref, knw_ref, k_ref, v_ref):
    xn = _rmsnorm(h, norm_ref[...]).astype(BF16)
    k = _dot(xn, wk_ref[...])
    ssq = _dot((k * k).astype(BF16), bd_ref[...])
    k_ref[...] = k * lax.rsqrt(ssq * (1.0 / HEAD_DIM) + EPS) * knw_ref[...]
    v_ref[...] = _dot(xn, wv_ref[...])


def _kv_kernel(h_ref, *refs):
    _kv_project(h_ref[...], *refs)


def _softmax_sink(s, sink):
    m = jnp.maximum(jnp.max(s, axis=-1, keepdims=True), sink)
    e = jnp.exp(s - m)
    denom = jnp.sum(e, axis=-1, keepdims=True) + jnp.exp(sink - m)
    return e * (1.0 / denom)


def _decode_attn_kernel(h_ref, knew_ref, vnew_ref, ck_ref, cv_ref,
                        nmix_ref, wq_ref, bd_ref, qnw_ref, bias_ref, sink_ref, kvmask_ref, rowa_ref,
                        o_ref, qn_ref, kxa_ref, kxb_ref, vxa_ref, vxb_ref, *, kpad):
    c = pl.program_id(0)

    @pl.when(c == 0)
    def _():
        qn_ref[...] = _q_heads(h_ref[...], nmix_ref[...], wq_ref, bd_ref, qnw_ref[...])
        zeros = jnp.zeros((kpad - WINDOW - PAIR_ROWS, KV_DIM), F32)
        for ref in (kxa_ref, kxb_ref, vxa_ref, vxb_ref):
            ref[WINDOW + PAIR_ROWS:kpad, :] = zeros

    kvmask = kvmask_ref[...]
    row_a = rowa_ref[...]
    row_b = 1.0 - row_a
    bias = bias_ref[...]
    sink = sink_ref[...]

    def rows_of(i):
        return pl.ds(pl.multiple_of((c * (SEQ_CHUNK // 2) + i) * PAIR_ROWS, PAIR_ROWS), PAIR_ROWS)

    def extended(ref, cache, new, seq):
        ref[0:WINDOW, :] = cache[seq]
        ref[WINDOW:WINDOW + PAIR_ROWS, :] = new
        return ref[...].astype(BF16)

    def scores(i):
        q8 = qn_ref[rows_of(i), :]
        qg = jnp.concatenate([q8[:, g * KV_DIM:(g + 1) * KV_DIM] for g in range(GROUP)], axis=0)
        qb = (jnp.concatenate([qg] * N_KV, axis=0) * kvmask).astype(BF16)
        knew = knew_ref[rows_of(i), :]
        sa = _dot_nt(qb, extended(kxa_ref, ck_ref, knew, 2 * i))
        sb = _dot_nt(qb, extended(kxb_ref, ck_ref, knew, 2 * i + 1))
        return sa * row_a + sb * row_b + bias

    nxt = scores(0)
    for i in range(SEQ_CHUNK // 2):
        s = nxt
        if i + 1 < SEQ_CHUNK // 2:
            nxt = scores(i + 1)
        pr = _softmax_sink(s, sink)
        vnew = vnew_ref[rows_of(i), :]
        o = (_dot((pr * row_a).astype(BF16), extended(vxa_ref, cv_ref, vnew, 2 * i))
             + _dot((pr * row_b).astype(BF16), extended(vxb_ref, cv_ref, vnew, 2 * i + 1))) * kvmask
        blk = GROUP * PAIR_ROWS
        og = o[0:blk] + o[blk:2 * blk] + o[2 * blk:3 * blk] + o[3 * blk:4 * blk]
        for g in range(GROUP):
            o_ref[rows_of(i), g * KV_DIM:(g + 1) * KV_DIM] = og[g * PAIR_ROWS:(g + 1) * PAIR_ROWS]


def _resident(shape, layer=None):
    zeros = (0,) * len(shape)
    if layer is None:
        return pl.BlockSpec(shape, lambda *_: zeros, pipeline_mode=pl.Buffered(1))
    return pl.BlockSpec((None,) + shape, lambda *_: (layer,) + zeros, pipeline_mode=pl.Buffered(1))


def _params(n_axes):
    return pltpu.CompilerParams(dimension_semantics=("arbitrary",) * n_axes,
                                vmem_limit_bytes=VMEM_LIMIT_BYTES)


def _ffn_weight_specs(i):
    return [_resident((1, D_MODEL), i), _resident((1, D_MODEL), i), _resident((D_MODEL, F2), i),
            _resident((CONV_W, F2), i), _resident((1, F2), i), _resident((D_FF, D_MODEL), i),
            _resident((D_MODEL, D_MODEL), i), _resident((PLE_DIM, D_MODEL), i)]


def _ffn_weights(W):
    return [W["norm_ffn"], W["norm_ple"], W["w_up"], W["conv_w"], W["conv_b"], W["w_down"],
            W["w_ple_gate"], W["w_ple_proj"]]


def _ffn_scratch_prompt(pre_c, rows):
    slabs = pltpu.VMEM((2 * FF_CHUNK // LANES, pre_c + rows, LANES), F32)
    return ([pltpu.VMEM((pre_c, F2), F32), pltpu.VMEM((D_MODEL // LANES, rows, LANES), F32)]
            + [slabs for _ in range(UP_AHEAD + 1)])


def _ffn_scratch(pre_c, rows):
    return ([pltpu.VMEM((pre_c, F2), F32)]
            + [pltpu.VMEM((pre_c + rows, 2 * FF_CHUNK), F32) for _ in range(UP_AHEAD + 1)])


def _prompt_specs(rows, n_tiles, total):
    mixer = lambda s: jnp.minimum(s, total - 1)
    ffn = lambda s: jnp.maximum(s - 1, 0)

    def make(which, per_sequence):
        def build(r, width, layer=None):
            lead = () if layer is None else (layer,)
            index = lambda s: lead + (which(s) // n_tiles, 0 if per_sequence else which(s) % n_tiles, 0)
            return pl.BlockSpec((None,) * (len(lead) + 1) + (r, width), index)
        return build

    return make(mixer, False), make(ffn, False), make(mixer, True), make(ffn, True)


def _kv_weight_specs():
    return [_resident((1, D_MODEL)), _resident((D_MODEL, KV_DIM)), _resident((D_MODEL, KV_DIM)),
            _resident((KV_DIM, KV_DIM)), _resident((1, KV_DIM))]


def _kv_weights(W, A):
    return [W["kv_norm"], A["w_k"], A["w_v"], A["bd_k"], A["knw"]]


def _pool_prompt_layer(h, p, W, i, *, rows):
    batch, length, _ = h.shape
    assert length % rows == 0
    tile = lambda width: pl.BlockSpec((None, rows, width), lambda b, t: (b, t, 0))
    state = lambda r, width: pl.BlockSpec((None, r, width), lambda b, t: (b, 0, 0))
    in_specs = [tile(D_MODEL), pl.BlockSpec((None, None, rows, PLE_DIM), lambda b, t: (i, b, t, 0)),
                _resident((1, D_MODEL), i),
                _resident((len(POOL_WINDOWS), POOL_GROUP, POOL_GROUP), i),
                _resident((1, D_MODEL), i)] + _ffn_weight_specs(i)
    args = [h, p, W["norm_mix"], W["w_pool"], W["pool_scale"]] + _ffn_weights(W)
    kern = functools.partial(_pool_prompt_kernel, rows=rows)
    return pl.pallas_call(
        kern,
        grid=(batch, length // rows),
        in_specs=in_specs,
        out_specs=[tile(D_MODEL), state(POOL_BUF, D_MODEL), state(CONV_BUF, F2)],
        out_shape=[jax.ShapeDtypeStruct(h.shape, F32),
                   jax.ShapeDtypeStruct((batch, POOL_BUF, D_MODEL), F32),
                   jax.ShapeDtypeStruct((batch, CONV_BUF, F2), F32)],
        scratch_shapes=[pltpu.VMEM((PROMPT_POOL_PRE, D_MODEL), F32)]
        + _ffn_scratch_prompt(PROMPT_CONV_PRE, rows),
        compiler_params=_params(2),
        name="pool_layer_prompt",
    )(*args)


def _attn_prompt_layer(h, p, k, v, W, A, i, *, rows):
    batch, length, _ = h.shape
    j = i - N_A
    n_tiles = length // rows
    total = batch * n_tiles
    blocks = rows // BLOCK
    assert n_tiles > 1 and length % rows == 0
    mixer_tile, ffn_tile, _, ffn_state = _prompt_specs(rows, n_tiles, total)
    mixer = lambda s: jnp.minimum(s, total - 1)
    prev = pl.BlockSpec(
        (None, BLOCK, KV_DIM),
        lambda s: (mixer(s) // n_tiles, jnp.maximum((mixer(s) % n_tiles) * blocks - 1, 0), 0))
    in_specs = [mixer_tile(rows, D_MODEL), ffn_tile(rows, PLE_DIM, i),
                prev, mixer_tile(rows, KV_DIM), prev, mixer_tile(rows, KV_DIM),
                _resident((1, D_MODEL), i), _resident((D_MODEL, D_MODEL), j),
                _resident((D_MODEL, LANES)), _resident((1, D_MODEL), j),
                _resident((2, N_HEADS, 2 * BLOCK, BLOCK)),
                _resident((N_KV, 1, GROUP * BLOCK), j), _resident((D_MODEL, D_MODEL), j)]
    in_specs += _ffn_weight_specs(i)
    args = [h, p, k, k, v, v, W["norm_mix"], A["w_q"], A["head_of_lane"], A["qnw"],
            A["bias_prompt"], A["sink_prompt"], A["w_o"]] + _ffn_weights(W)
    kern = functools.partial(_attn_prompt_kernel, rows=rows, n_tiles=n_tiles, total=total)
    khalf = pltpu.VMEM((N_KV // 2, rows + BLOCK, LANES), BF16)
    vhalf = pltpu.VMEM((N_KV // 2, 2 * LANES, rows + BLOCK), BF16)
    return pl.pallas_call(
        kern,
        grid=(total + 1,),
        in_specs=in_specs,
        out_specs=[ffn_tile(rows, D_MODEL), ffn_state(CONV_BUF, F2)],
        out_shape=[jax.ShapeDtypeStruct(h.shape, F32),
                   jax.ShapeDtypeStruct((batch, CONV_BUF, F2), F32)],
        scratch_shapes=[pltpu.VMEM((2, rows, D_MODEL), F32), khalf, khalf, vhalf, vhalf,
                        pltpu.VMEM((rows, D_MODEL), BF16)] + _ffn_scratch_prompt(PROMPT_CONV_PRE, rows),
        compiler_params=_params(1),
        name="attn_layer_prompt",
    )(*args)


def _decode_specs(rows, step):
    def build(r, width, layer=None):
        lead = () if layer is None else (layer,)
        return pl.BlockSpec((None,) * len(lead) + (r, step, width), lambda t: lead + (0, t, 0))
    return build


def _pool_decode_layer(h, p, ppre, cpre, W, i, *, rows, step):
    steps, n_seq, _ = h.shape
    block = _decode_specs(rows, step)
    in_specs = [block(steps, D_MODEL), block(steps, PLE_DIM, i), block(POOL_BUF, D_MODEL, i),
                block(CONV_BUF, F2, i), _resident((1, D_MODEL), i),
                _resident((len(POOL_WINDOWS), POOL_GROUP, POOL_GROUP), i),
                _resident((1, D_MODEL), i)] + _ffn_weight_specs(i)
    kern = functools.partial(_pool_decode_kernel, rows=rows, step=step)
    return pl.pallas_call(
        kern,
        grid=(n_seq // step,),
        in_specs=in_specs,
        out_specs=[block(steps, D_MODEL), block(POOL_BUF, D_MODEL), block(CONV_BUF, F2)],
        out_shape=[jax.ShapeDtypeStruct(h.shape, F32),
                   jax.ShapeDtypeStruct((POOL_BUF, n_seq, D_MODEL), F32),
                   jax.ShapeDtypeStruct((CONV_BUF, n_seq, F2), F32)],
        scratch_shapes=_ffn_scratch(CONV_BUF * step, rows),
        compiler_params=_params(1),
        name="pool_layer_decode",
    )(h, p, ppre, cpre, W["norm_mix"], W["w_pool"], W["pool_scale"], *_ffn_weights(W))


def _attn_post_decode_layer(h, o, p, cpre, W, A, i, *, rows, step):
    steps, n_seq, _ = h.shape
    j = i - N_A
    block = _decode_specs(rows, step)
    kern = functools.partial(_attn_post_decode_kernel, rows=rows, step=step)
    return pl.pallas_call(
        kern,
        grid=(n_seq // step,),
        in_specs=[block(steps, D_MODEL), block(steps, D_MODEL), block(steps, PLE_DIM, i),
                  block(CONV_BUF, F2, i), _resident((D_MODEL, D_MODEL), j)] + _ffn_weight_specs(i),
        out_specs=[block(steps, D_MODEL), block(CONV_BUF, F2)],
        out_shape=[jax.ShapeDtypeStruct(h.shape, F32),
                   jax.ShapeDtypeStruct((CONV_BUF, n_seq, F2), F32)],
        scratch_shapes=_ffn_scratch(CONV_BUF * step, rows),
        compiler_params=_params(1),
        name="attn_post_layer_decode",
    )(h, o, p, cpre, A["w_o"], *_ffn_weights(W))


def _shared_kv(h2d, W, A, *, rows):
    total = h2d.shape[0]
    tile = lambda width: pl.BlockSpec((rows, width), lambda t: (t, 0))
    return pl.pallas_call(
        _kv_kernel,
        grid=(total // rows,),
        in_specs=[tile(D_MODEL)] + _kv_weight_specs(),
        out_specs=[tile(KV_DIM), tile(KV_DIM)],
        out_shape=[jax.ShapeDtypeStruct((total, KV_DIM), F32)] * 2,
        compiler_params=_params(1),
        name="shared_kv",
    )(h2d, *_kv_weights(W, A))


def _decode_attn(h_bm, knew, vnew, cache_k, cache_v, W, A, i):
    rows = h_bm.shape[0]
    n_seq = cache_k.shape[0]
    j = i - N_A
    kpad = A["bias_decode"].shape[1]
    qrows = N_KV * GROUP * PAIR_ROWS
    full = lambda r, width: pl.BlockSpec((r, width), lambda c: (0, 0))
    cache = pl.BlockSpec((SEQ_CHUNK, WINDOW, KV_DIM), lambda c: (c, 0, 0))
    kern = functools.partial(_decode_attn_kernel, kpad=kpad)
    ext = pltpu.VMEM((kpad, KV_DIM), F32)
    return pl.pallas_call(
        kern,
        grid=(n_seq // SEQ_CHUNK,),
        in_specs=[full(rows, D_MODEL), full(rows, KV_DIM), full(rows, KV_DIM), cache, cache,
                  _resident((1, D_MODEL), i), _resident((D_MODEL, D_MODEL), j),
                  _resident((D_MODEL, D_MODEL)), _resident((1, D_MODEL), j),
                  _resident((qrows, kpad)), _resident((qrows, 1), j),
                  _resident((qrows, KV_DIM)), _resident((qrows, 1))],
        out_specs=full(rows, D_MODEL),
        out_shape=jax.ShapeDtypeStruct((rows, D_MODEL), F32),
        scratch_shapes=[pltpu.VMEM((rows, D_MODEL), F32), ext, ext, ext, ext],
        compiler_params=_params(1),
        name="decode_attn",
    )(h_bm, knew, vnew, cache_k, cache_v, W["norm_mix"], A["w_q"], A["bd_q"], A["qnw"],
      A["bias_decode"], A["sink_decode"], A["kvmask"], A["row_a"])


def _t5_bucket(d):
    n = jnp.maximum(d, 0)
    max_exact = N_BUCKETS // 2
    nf = jnp.maximum(n, 1).astype(F32)
    large = max_exact + (jnp.log(nf / max_exact) / math.log(MAX_DISTANCE / max_exact)
                         * (N_BUCKETS - max_exact)).astype(jnp.int32)
    large = jnp.minimum(large, N_BUCKETS - 1)
    return jnp.where(n < max_exact, n, large)


def _block_diag_ones(width):
    idx = np.arange(width) // HEAD_DIM
    return jnp.asarray(idx[:, None] == idx[None, :], BF16)


def _banded_bias(tab):
    heads = tab.shape[1]
    period = 3 * BLOCK + 1
    base = jnp.concatenate([jnp.full((heads, BLOCK - 1), NEG_INF, F32), tab.T,
                            jnp.full((heads, period - 2 * BLOCK + 1), NEG_INF, F32)], axis=1)
    flat = jnp.tile(base, (1, 2 * BLOCK))[:, :2 * BLOCK * (period - 1)]
    skew = flat.reshape(heads, 2 * BLOCK, period - 1)
    return skew[:, :, 2 * BLOCK - 1:3 * BLOCK - 1]


def _attention_tables(rel_bias, sinks):
    A = {}
    tab = rel_bias[_t5_bucket(jnp.arange(WINDOW))].astype(F32)

    steady = _banded_bias(tab)
    has_prev = (np.arange(2 * BLOCK) >= BLOCK)[None, :, None]
    A["bias_prompt"] = jnp.stack([jnp.where(has_prev, steady, NEG_INF), steady])

    tab3 = tab.reshape(WINDOW, N_KV, GROUP)
    steps = PAIR_ROWS // 2
    kpad = -(-(WINDOW + PAIR_ROWS) // 16) * 16
    col = np.arange(kpad)[None, :]
    r = np.arange(PAIR_ROWS)[:, None]
    step_of = r % steps
    seq_b = r >= steps
    keyidx = np.where(col < WINDOW, col, WINDOW + (col - WINDOW) % steps)
    own = (col < WINDOW) | ((col < WINDOW + PAIR_ROWS) & (((col - WINDOW) >= steps) == seq_b))
    dd = step_of + WINDOW - keyidx
    dvalid = own & (dd >= 0) & (dd < WINDOW)
    ddc = np.clip(dd, 0, WINDOW - 1)
    dec = jnp.where(dvalid[:, :, None, None], tab3[ddc], NEG_INF)
    A["bias_decode"] = dec.transpose(2, 3, 0, 1).reshape(N_KV * GROUP * PAIR_ROWS, kpad)
    A["row_a"] = jnp.asarray(np.tile(~seq_b, (N_KV * GROUP, 1)).reshape(-1, 1), F32)
    lane_kv = np.arange(KV_DIM)[None, :] // HEAD_DIM
    row_kv = np.arange(N_KV * GROUP * PAIR_ROWS)[:, None] // (GROUP * PAIR_ROWS)
    A["kvmask"] = jnp.asarray(lane_kv == row_kv, F32)

    sk = sinks.astype(F32).reshape(-1, N_KV, GROUP)
    A["sink_prompt"] = jnp.repeat(sk, BLOCK, axis=2)[:, :, None, :]
    A["sink_decode"] = jnp.repeat(sk.reshape(sk.shape[0], -1), PAIR_ROWS, axis=1)[..., None]
    return A


def _prepare(W):
    A = _attention_tables(W["rel_bias"], W["sinks"])
    n_b = W["w_q"].shape[0]
    A["w_q"] = (W["w_q"].reshape(n_b, D_MODEL, N_KV, GROUP, HEAD_DIM).transpose(0, 1, 3, 2, 4)
                .reshape(n_b, D_MODEL, D_MODEL).astype(BF16))
    A["w_o"] = (W["w_o"].reshape(n_b, N_KV, GROUP, HEAD_DIM, D_MODEL).transpose(0, 2, 1, 3, 4)
                .reshape(n_b, D_MODEL, D_MODEL).astype(BF16))
    A["qnw"] = jnp.tile(W["q_norm"], (1, N_HEADS)).reshape(n_b, 1, D_MODEL)
    A["knw"] = jnp.tile(W["k_norm"], N_KV).reshape(1, KV_DIM)
    A["w_k"] = W["w_k"].astype(BF16)
    A["w_v"] = W["w_v"].astype(BF16)
    A["bd_q"] = _block_diag_ones(D_MODEL)
    A["head_of_lane"] = jnp.asarray(
        (np.arange(D_MODEL) // HEAD_DIM)[:, None] == np.arange(LANES)[None, :], BF16)
    A["bd_k"] = _block_diag_ones(KV_DIM)
    V = dict(W)
    for name in ("w_pool", "w_up", "w_down", "w_ple_gate", "w_ple_proj"):
        V[name] = W[name].astype(BF16)
    for name in ("norm_mix", "norm_ffn", "norm_ple", "pool_scale", "conv_b"):
        V[name] = W[name][:, None, :]
    V["kv_norm"] = W["kv_norm"][None, :]
    return V, A


def kernel(x_prompt, x_sample, p_prompt, p_sample, state_pool, state_conv, cache_k, cache_v,
           norm_mix, norm_ffn, norm_ple, w_pool, pool_scale, kv_norm, w_k, w_v, k_norm,
           w_q, q_norm, sinks, w_o, rel_bias, w_up, conv_w, conv_b, w_down, w_ple_gate, w_ple_proj):
    W, A = _prepare(dict(
        norm_mix=norm_mix, norm_ffn=norm_ffn, norm_ple=norm_ple, w_pool=w_pool,
        pool_scale=pool_scale, kv_norm=kv_norm, w_k=w_k, w_v=w_v, k_norm=k_norm, w_q=w_q,
        q_norm=q_norm, sinks=sinks, w_o=w_o, rel_bias=rel_bias, w_up=w_up, conv_w=conv_w,
        conv_b=conv_b, w_down=w_down, w_ple_gate=w_ple_gate, w_ple_proj=w_ple_proj))
    batch, seq, _ = x_prompt.shape
    n_seq, steps, _ = x_sample.shape

    h = x_prompt
    pool_p, conv_p = [], []
    for i in range(N_A):
        h, ps, cs = _pool_prompt_layer(h, p_prompt, W, i, rows=PROMPT_TILE)
        pool_p.append(ps)
        conv_p.append(cs)
    k_p, v_p = _shared_kv(h.reshape(batch * seq, D_MODEL), W, A, rows=KV_TILE)
    k_p = k_p.reshape(batch, seq, KV_DIM)
    v_p = v_p.reshape(batch, seq, KV_DIM)
    for i in range(N_A, DEPTH):
        h, cs = _attn_prompt_layer(h, p_prompt, k_p, v_p, W, A, i, rows=PROMPT_TILE)
        conv_p.append(cs)
    y_prompt = h

    swap = lambda x: jnp.swapaxes(x, -3, -2)
    tile_rows = steps * DECODE_SEQ_TILE
    total = steps * n_seq
    hs = swap(x_sample)
    ps_sw, ppre_sw, cpre_sw = swap(p_sample), swap(state_pool), swap(state_conv)
    pool_s, conv_s = [], []
    for i in range(N_A):
        hs, ps, cs = _pool_decode_layer(hs, ps_sw, ppre_sw, cpre_sw, W, i, rows=tile_rows,
                                        step=DECODE_SEQ_TILE)
        pool_s.append(ps)
        conv_s.append(cs)
    hs_bm = swap(hs).reshape(total, D_MODEL)
    k_new, v_new = _shared_kv(hs_bm, W, A, rows=total)
    ck = cache_k.reshape(n_seq, WINDOW, KV_DIM)
    cv = cache_v.reshape(n_seq, WINDOW, KV_DIM)
    for i in range(N_A, DEPTH):
        o_bm = _decode_attn(hs_bm, k_new, v_new, ck, cv, W, A, i)
        hs, cs = _attn_post_decode_layer(hs, swap(o_bm.reshape(n_seq, steps, D_MODEL)), ps_sw,
                                         cpre_sw, W, A, i, rows=tile_rows, step=DECODE_SEQ_TILE)
        conv_s.append(cs)
        hs_bm = swap(hs).reshape(total, D_MODEL)
    y_sample = hs_bm.reshape(n_seq, steps, D_MODEL)

    def window(cache, new):
        ext = jnp.concatenate([cache, new.reshape(n_seq, steps, N_KV, HEAD_DIM)], axis=1)
        return ext[:, -WINDOW:]

    kp4 = k_p[:, -WINDOW:].reshape(batch, WINDOW, N_KV, HEAD_DIM)
    vp4 = v_p[:, -WINDOW:].reshape(batch, WINDOW, N_KV, HEAD_DIM)
    return (y_prompt, y_sample, jnp.stack(pool_p), swap(jnp.stack(pool_s)), jnp.stack(conv_p),
            swap(jnp.stack(conv_s)), kp4, window(cache_k, k_new), vp4, window(cache_v, v_new))
```

```python
import functools
import math

import numpy as np
import jax
import jax.numpy as jnp
from jax import lax
from jax.experimental import pallas as pl
from jax.experimental.pallas import tpu as pltpu

D_MODEL = 1024
DEPTH = 4
N_A = DEPTH // 2
POOL_WINDOWS = (2, 4, 8, 16)
POOL_GROUP = D_MODEL // len(POOL_WINDOWS)
POOL_BUF = max(POOL_WINDOWS) - 1
HEAD_DIM = 64
N_HEADS = D_MODEL // HEAD_DIM
N_KV = 4
GROUP = N_HEADS // N_KV
KV_DIM = N_KV * HEAD_DIM
WINDOW = 128
BLOCK = WINDOW
N_BUCKETS = 32
MAX_DISTANCE = 128
D_FF = 2816
F2 = 2 * D_FF
CONV_W = 3
CONV_BUF = CONV_W - 1
PLE_DIM = 256
EPS = 1e-6
PAST_LEN = 8192

LANES = 128
SUBLANES = 8
BF16_SUBLANES = 16
VMEM_LIMIT_BYTES = 58 * 1024 * 1024

PROMPT_TILE = 256
KV_TILE = 2048
FF_CHUNK = 512
UP_AHEAD = 2
ROW_PHASES = 4
DECODE_SEQ_TILE = 64
SEQ_CHUNK = 16
PAIR_ROWS = 2 * 4
PROMPT_POOL_PRE = 2 * SUBLANES
PROMPT_CONV_PRE = SUBLANES

F32 = jnp.float32
BF16 = jnp.bfloat16
NEG_INF = float("-inf")


def _rmsnorm(x, g):
    ms = jnp.mean(x * x, axis=-1, keepdims=True)
    return x * lax.rsqrt(ms + EPS) * g


def _gelu_tanh(x):
    c = np.float32(math.sqrt(2.0 / math.pi))
    cdf = 0.5 * (1.0 + jnp.tanh(c * (x + 0.044715 * (x * x * x))))
    return x * cdf


def _sigmoid(x):
    return 1.0 / (1.0 + jnp.exp(-x))


def _dot(a, b):
    return jnp.dot(a, b, preferred_element_type=F32)


def _dot_nt(a, b):
    return lax.dot_general(a, b, (((1,), (1,)), ((), ())), preferred_element_type=F32)


def _pool_stage(h, norm_g, carried, wpool_ref, scale, inv_cnt, pre, step, rows):
    xn = _rmsnorm(h, norm_g)
    ext = jnp.concatenate([carried, xn], axis=0)
    state = ext[pre + rows - POOL_BUF * step:pre + rows, :]
    carry = ext[rows:rows + pre, :]
    outs = []
    for g, w in enumerate(POOL_WINDOWS):
        cols = slice(g * POOL_GROUP, (g + 1) * POOL_GROUP)
        s = ext[:, cols]
        width = 1
        while width < w:
            s = s + pltpu.roll(s, width * step, 0)
            width *= 2
        d = s[pre:pre + rows, :] * inv_cnt[g] - xn[:, cols]
        outs.append(_dot(d.astype(BF16), wpool_ref[g]))
    return h + jnp.concatenate(outs, axis=1) * scale, state, carry


def _ffn_begin(h, norm_g, ucarry_ref, ubufs, wup_ref, cw_ref, cb_ref, wdown_ref, pre, step, rows):
    xn = _rmsnorm(h, norm_g).astype(BF16)
    bounds = list(range(0, D_FF, FF_CHUNK)) + [D_FF]
    n_chunks = len(bounds) - 1
    halves = lambda c: [slice(base + bounds[c], base + bounds[c + 1]) for base in (0, D_FF)]
    local = lambda c: [slice(j * FF_CHUNK, j * FF_CHUNK + bounds[c + 1] - bounds[c]) for j in range(2)]

    def up(c):
        ub = ubufs[c % len(ubufs)]
        for cols, loc in zip(halves(c), local(c)):
            ub[0:pre, loc] = ucarry_ref[:, cols]
            ub[pre:pre + rows, loc] = _dot(xn, wup_ref[:, cols])

    for c in range(min(UP_AHEAD, n_chunks)):
        up(c)

    def finish(live=None, between=()):
        f = None
        for c in range(n_chunks):
            if c + UP_AHEAD < n_chunks:
                up(c + UP_AHEAD)
            ub = ubufs[c % len(ubufs)]
            conv = []
            for cols, loc in zip(halves(c), local(c)):
                ue = ub[:, loc]
                tail = ue[rows:rows + pre, :]
                ucarry_ref[:, cols] = tail if live is None else jnp.where(live, tail, ue[0:pre, :])
                acc = cb_ref[:, cols] + pltpu.roll(ue, 2 * step, 0)[pre:pre + rows, :] * cw_ref[0:1, cols]
                acc = acc + pltpu.roll(ue, step, 0)[pre:pre + rows, :] * cw_ref[1:2, cols]
                conv.append(acc + ue[pre:pre + rows, :] * cw_ref[2:3, cols])
            a = (_gelu_tanh(conv[0]) * conv[1]).astype(BF16)
            part = _dot(a, wdown_ref[bounds[c]:bounds[c + 1], :])
            f = part if f is None else f + part
            if c < len(between):
                between[c]()
        return h + f

    return finish


def _ffn_begin_prompt(h, norm_g, ucarry_ref, ubufs, fs_ref, wup_ref, cw_ref, cb_ref, wdown_ref,
                      pre, rows, live=None):
    xn = _rmsnorm(h, norm_g).astype(BF16)
    bounds = list(range(0, D_FF, FF_CHUNK)) + [D_FF]
    n_chunks = len(bounds) - 1
    slabs_per_half = FF_CHUNK // LANES
    part_rows = rows // ROW_PHASES

    def up(c):
        ub = ubufs[c % len(ubufs)]
        for half, base in enumerate((0, D_FF)):
            cols = slice(base + bounds[c], base + bounds[c + 1])
            u = _dot(xn, wup_ref[:, cols])
            old = ucarry_ref[:, cols]
            tail = u[rows - pre:rows, :]
            ucarry_ref[:, cols] = tail if live is None else jnp.where(live, tail, old)
            for k in range((bounds[c + 1] - bounds[c]) // LANES):
                lanes = slice(k * LANES, (k + 1) * LANES)
                ub[half * slabs_per_half + k, 0:pre, :] = old[:, lanes]
                ub[half * slabs_per_half + k, pre:pre + rows, :] = u[:, lanes]

    for c in range(min(UP_AHEAD, n_chunks)):
        up(c)

    def finish(between=()):
        f = None
        for c in range(n_chunks):
            ub = ubufs[c % len(ubufs)]
            phases = []
            for s in range(ROW_PHASES):
                slabs = []
                for k in range((bounds[c + 1] - bounds[c]) // LANES):
                    conv = []
                    for half, base in enumerate((0, D_FF)):
                        cols = slice(base + bounds[c] + k * LANES, base + bounds[c] + (k + 1) * LANES)
                        slab = half * slabs_per_half + k
                        tap = lambda d: ub[slab, pl.ds(pre + s - d, part_rows, stride=ROW_PHASES), :]
                        acc = cb_ref[:, cols] + tap(2) * cw_ref[0:1, cols]
                        acc = acc + tap(1) * cw_ref[1:2, cols]
                        conv.append(acc + tap(0) * cw_ref[2:3, cols])
                    slabs.append((_gelu_tanh(conv[0]) * conv[1]).astype(BF16))
                phases.append(jnp.concatenate(slabs, axis=1))
            a = jnp.concatenate(phases, axis=0)
            if c + UP_AHEAD < n_chunks:
                up(c + UP_AHEAD)
            part = _dot(a, wdown_ref[bounds[c]:bounds[c + 1], :])
            f = part if f is None else f + part
            if c < len(between):
                between[c]()
        for s in range(ROW_PHASES):
            for k in range(D_MODEL // LANES):
                fs_ref[k, pl.ds(s, part_rows, stride=ROW_PHASES), :] = (
                    f[s * part_rows:(s + 1) * part_rows, k * LANES:(k + 1) * LANES])
        return h + jnp.concatenate([fs_ref[k] for k in range(D_MODEL // LANES)], axis=1)

    return finish


def _ple(h, norm_g, p, wg_ref, wp_ref):
    xn = _rmsnorm(h, norm_g).astype(BF16)
    gate = _sigmoid(_dot(xn, wg_ref[...]))
    return h + gate * _dot(p.astype(BF16), wp_ref[...])


def _q_heads(h, norm_g, wq_ref, bd_ref, qnw):
    xn = _rmsnorm(h, norm_g).astype(BF16)
    q = _dot(xn, wq_ref[...])
    ssq = _dot((q * q).astype(BF16), bd_ref[...])
    return q * lax.rsqrt(ssq * (1.0 / HEAD_DIM) + EPS) * (qnw * HEAD_DIM ** -0.5)


def _banded_attention_pieces(h, first_tile, kv_window, refs, store_result, rows):
    (nmix_ref, wq_ref, he_ref, qnw_ref, bias_ref,
     sink_ref, wo_ref, klo_ref, khi_ref, vlo_ref, vhi_ref, o_ref) = refs
    low = lax.broadcasted_iota(jnp.int32, (1, LANES), 1) < HEAD_DIM
    top = lax.broadcasted_iota(jnp.int32, (LANES, 1), 0) < HEAD_DIM
    units = [(qb, pr) for qb in range(rows // BLOCK) for pr in range(N_KV // 2)]
    live = {}

    def scores(qb, pr):
        qrows = slice(qb * BLOCK, (qb + 1) * BLOCK)
        krows = slice(qb * BLOCK, (qb + 2) * BLOCK)
        qn = live["qn"]
        qs = jnp.concatenate(
            [qn[qrows, g * KV_DIM + pr * LANES:g * KV_DIM + (pr + 1) * LANES] for g in range(GROUP)],
            axis=0)
        kc = jnp.concatenate([klo_ref[pr, krows, :], khi_ref[pr, krows, :]], axis=0)
        return _dot_nt(kc, qs)

    def project():
        xn = _rmsnorm(h, nmix_ref[...]).astype(BF16)
        q = _dot(xn, wq_ref[...])
        ssq = _dot((q * q).astype(BF16), he_ref[...])
        live["rinv_t"] = lax.rsqrt(ssq * (1.0 / HEAD_DIM) + EPS).T
        live["qn"] = (q * HEAD_DIM ** -0.5).astype(BF16)
        kw, vw = kv_window()
        kw = kw * qnw_ref[:, 0:KV_DIM]
        vwt = vw.T
        ones_top = jnp.broadcast_to(jnp.where(top, 1.0, 0.0), (LANES, rows + BLOCK))
        for pr in range(N_KV // 2):
            lanes = slice(pr * LANES, (pr + 1) * LANES)
            klo_ref[pr] = jnp.where(low, kw[:, lanes], 0.0).astype(BF16)
            khi_ref[pr] = jnp.where(low, 0.0, kw[:, lanes]).astype(BF16)
            vlo_ref[pr] = jnp.concatenate([jnp.where(top, vwt[lanes, :], 0.0), ones_top],
                                          axis=0).astype(BF16)
            vhi_ref[pr] = jnp.concatenate([jnp.where(top, 0.0, vwt[lanes, :]), 1.0 - ones_top],
                                          axis=0).astype(BF16)
        live["s"] = scores(*units[0])

    def unit(n):
        qb, pr = units[n]
        s = live["s"]
        if n + 1 < len(units):
            live["s"] = scores(*units[n + 1])
        qrows = slice(qb * BLOCK, (qb + 1) * BLOCK)
        krows = slice(qb * BLOCK, (qb + 2) * BLOCK)
        table = jnp.where(first_tile, 0, 1) if qb == 0 else 1
        es, sinkterms = [], []
        for half in range(2):
            kv = 2 * pr + half
            bias = jnp.concatenate([bias_ref[table, kv * GROUP + g] for g in range(GROUP)], axis=1)
            rinv = jnp.concatenate(
                [live["rinv_t"][g * N_KV + kv:g * N_KV + kv + 1, qrows] for g in range(GROUP)], axis=1)
            sh = s[half * 2 * BLOCK:(half + 1) * 2 * BLOCK, :] * rinv + bias
            sink = sink_ref[kv]
            m = jnp.maximum(jnp.max(sh, axis=0, keepdims=True), sink)
            es.append(jnp.exp(sh - m).astype(BF16))
            sinkterms.append(jnp.exp(sink - m))
        vct = jnp.concatenate([vlo_ref[pr, :, krows], vhi_ref[pr, :, krows]], axis=1)
        ov = _dot(vct, jnp.concatenate(es, axis=0))
        denom = ov[LANES:, :] + jnp.where(top, sinkterms[0], sinkterms[1])
        o = (ov[:LANES, :] * (1.0 / denom)).T
        for g in range(GROUP):
            o_ref[qrows, g * KV_DIM + pr * LANES:g * KV_DIM + (pr + 1) * LANES] = (
                o[g * BLOCK:(g + 1) * BLOCK, :].astype(BF16))

    def output():
        store_result(h + _dot(o_ref[...], wo_ref[...]))

    return [project] + [functools.partial(unit, n) for n in range(len(units))] + [output]


def _pipeline_flags(n_tiles, total):
    s = pl.program_id(0)
    mixer_live = s < total
    ffn_live = s >= 1
    mixer_first = lax.rem(s, n_tiles) == 0
    ffn_first = lax.rem(s + n_tiles - 1, n_tiles) == 0
    return s, mixer_live, ffn_live, mixer_first, ffn_first


def _pool_prompt_kernel(h_ref, p_ref, nmix_ref, wpool_ref, pscale_ref,
                        nffn_ref, nple_ref, wup_ref, cw_ref, cb_ref, wdown_ref, wg_ref, wp_ref,
                        hout_ref, pstate_ref, cstate_ref,
                        pcarry_ref, ucarry_ref, fs_ref, *ubufs, rows):
    t = pl.program_id(1)
    pre_p, pre_c = PROMPT_POOL_PRE, PROMPT_CONV_PRE

    @pl.when(t == 0)
    def _():
        pcarry_ref[...] = jnp.zeros((pre_p, D_MODEL), F32)
        ucarry_ref[...] = jnp.zeros((pre_c, F2), F32)

    pos = t * rows + lax.broadcasted_iota(jnp.int32, (rows, POOL_GROUP), 0)
    inv_cnt = [1.0 / jnp.minimum(pos + 1, w).astype(F32) for w in POOL_WINDOWS]
    h, state, carry = _pool_stage(h_ref[...], nmix_ref[...], pcarry_ref[...], wpool_ref,
                                  pscale_ref[...], inv_cnt, pre_p, 1, rows)
    pstate_ref[...] = state
    pcarry_ref[...] = carry
    h = _ffn_begin_prompt(h, nffn_ref[...], ucarry_ref, list(ubufs), fs_ref, wup_ref, cw_ref, cb_ref,
                          wdown_ref, pre_c, rows)()
    cstate_ref[...] = ucarry_ref[pre_c - CONV_BUF:pre_c, :]
    hout_ref[...] = _ple(h, nple_ref[...], p_ref[...], wg_ref, wp_ref)


def _attn_prompt_kernel(h_ref, p_ref, kprev_ref, kcur_ref, vprev_ref, vcur_ref,
                        nmix_ref, wq_ref, he_ref, qnw_ref, bias_ref, sink_ref, wo_ref,
                        nffn_ref, nple_ref, wup_ref, cw_ref, cb_ref, wdown_ref, wg_ref, wp_ref,
                        hout_ref, cstate_ref,
                        hmid_ref, klo_ref, khi_ref, vlo_ref, vhi_ref, o_ref, ucarry_ref, fs_ref,
                        *ubufs, rows, n_tiles, total):
    s, _, ffn_live, mixer_first, ffn_first = _pipeline_flags(n_tiles, total)
    pre_c = PROMPT_CONV_PRE

    @pl.when(s == 0)
    def _():
        hmid_ref[...] = jnp.zeros(hmid_ref.shape, F32)

    def kv_window():
        return (jnp.concatenate([kprev_ref[...], kcur_ref[...]], axis=0),
                jnp.concatenate([vprev_ref[...], vcur_ref[...]], axis=0))

    @pl.when(jnp.logical_or(ffn_first, s == 0))
    def _():
        ucarry_ref[...] = jnp.zeros((pre_c, F2), F32)

    slot = lax.rem(s, 2)
    finish = _ffn_begin_prompt(hmid_ref[1 - slot], nffn_ref[...], ucarry_ref, list(ubufs), fs_ref,
                               wup_ref, cw_ref, cb_ref, wdown_ref, pre_c, rows, live=ffn_live)

    def store_result(value):
        hmid_ref[slot] = value

    pieces = _banded_attention_pieces(
        h_ref[...], mixer_first, kv_window,
        (nmix_ref, wq_ref, he_ref, qnw_ref, bias_ref,
         sink_ref, wo_ref, klo_ref, khi_ref, vlo_ref, vhi_ref, o_ref), store_result, rows)
    n_between = -(-D_FF // FF_CHUNK) - 1
    lead = max(len(pieces) - n_between, 1)
    for piece in pieces[:lead]:
        piece()
    h = finish(pieces[lead:])
    cstate_ref[...] = ucarry_ref[pre_c - CONV_BUF:pre_c, :]
    hout_ref[...] = _ple(h, nple_ref[...], p_ref[...], wg_ref, wp_ref)


def _decode_ffn_ple(h, refs, pre_c, step, rows):
    (p_ref, cpre_ref, nffn_ref, nple_ref, wup_ref, cw_ref, cb_ref, wdown_ref, wg_ref, wp_ref,
     hout_ref, cstate_ref, ucarry_ref, ubufs) = refs
    ucarry_ref[...] = cpre_ref[...].reshape(pre_c, F2)
    h = _ffn_begin(h, nffn_ref[...], ucarry_ref, ubufs, wup_ref, cw_ref, cb_ref, wdown_ref,
                   pre_c, step, rows)()
    cstate_ref[...] = ucarry_ref[...].reshape(cstate_ref.shape)
    p = p_ref[...].reshape(rows, PLE_DIM)
    hout_ref[...] = _ple(h, nple_ref[...], p, wg_ref, wp_ref).reshape(hout_ref.shape)


def _pool_decode_kernel(h_ref, p_ref, ppre_ref, cpre_ref, nmix_ref, wpool_ref, pscale_ref,
                        nffn_ref, nple_ref, wup_ref, cw_ref, cb_ref, wdown_ref, wg_ref, wp_ref,
                        hout_ref, pstate_ref, cstate_ref, ucarry_ref, *ubufs, rows, step):
    pre_p, pre_c = POOL_BUF * step, CONV_BUF * step
    inv_cnt = [1.0 / min(PAST_LEN + 1, w) for w in POOL_WINDOWS]
    h, state, _ = _pool_stage(h_ref[...].reshape(rows, D_MODEL), nmix_ref[...],
                              ppre_ref[...].reshape(pre_p, D_MODEL), wpool_ref, pscale_ref[...],
                              inv_cnt, pre_p, step, rows)
    pstate_ref[...] = state.reshape(pstate_ref.shape)
    _decode_ffn_ple(h, (p_ref, cpre_ref, nffn_ref, nple_ref, wup_ref, cw_ref, cb_ref, wdown_ref,
                        wg_ref, wp_ref, hout_ref, cstate_ref, ucarry_ref, list(ubufs)),
                    pre_c, step, rows)


def _attn_post_decode_kernel(h_ref, o_ref, p_ref, cpre_ref, wo_ref,
                             nffn_ref, nple_ref, wup_ref, cw_ref, cb_ref, wdown_ref, wg_ref, wp_ref,
                             hout_ref, cstate_ref, ucarry_ref, *ubufs, rows, step):
    h = h_ref[...].reshape(rows, D_MODEL) + _dot(o_ref[...].reshape(rows, D_MODEL).astype(BF16),
                                                 wo_ref[...])
    _decode_ffn_ple(h, (p_ref, cpre_ref, nffn_ref, nple_ref, wup_ref, cw_ref, cb_ref, wdown_ref,
                        wg_ref, wp_ref, hout_ref, cstate_ref, ucarry_ref, list(ubufs)),
                    CONV_BUF * step, step, rows)


def _kv_project(h, norm_ref, wk_ref, wv_ref, bd_ref, knw_ref, k_ref, v_ref):
    xn = _rmsnorm(h, norm_ref[...]).astype(BF16)
    k = _dot(xn, wk_ref[...])
    ssq = _dot((k * k).astype(BF16), bd_ref[...])
    k_ref[...] = k * lax.rsqrt(ssq * (1.0 / HEAD_DIM) + EPS) * knw_ref[...]
    v_ref[...] = _dot(xn, wv_ref[...])


def _kv_kernel(h_ref, *refs):
    _kv_project(h_ref[...], *refs)


def _softmax_sink(s, sink):
    m = jnp.maximum(jnp.max(s, axis=-1, keepdims=True), sink)
    e = jnp.exp(s - m)
    denom = jnp.sum(e, axis=-1, keepdims=True) + jnp.exp(sink - m)
    return e * (1.0 / denom)


def _decode_attn_kernel(h_ref, knew_ref, vnew_ref, ck_ref, cv_ref,
                        nmix_ref, wq_ref, bd_ref, qnw_ref, bias_ref, sink_ref, kvmask_ref, rowa_ref,
                        o_ref, qn_ref, kxa_ref, kxb_ref, vxa_ref, vxb_ref, *, kpad):
    c = pl.program_id(0)

    @pl.when(c == 0)
    def _():
        qn_ref[...] = _q_heads(h_ref[...], nmix_ref[...], wq_ref, bd_ref, qnw_ref[...])
        zeros = jnp.zeros((kpad - WINDOW - PAIR_ROWS, KV_DIM), F32)
        for ref in (kxa_ref, kxb_ref, vxa_ref, vxb_ref):
            ref[WINDOW + PAIR_ROWS:kpad, :] = zeros

    kvmask = kvmask_ref[...]
    row_a = rowa_ref[...]
    row_b = 1.0 - row_a
    bias = bias_ref[...]
    sink = sink_ref[...]

    def rows_of(i):
        return pl.ds(pl.multiple_of((c * (SEQ_CHUNK // 2) + i) * PAIR_ROWS, PAIR_ROWS), PAIR_ROWS)

    def extended(ref, cache, new, seq):
        ref[0:WINDOW, :] = cache[seq]
        ref[WINDOW:WINDOW + PAIR_ROWS, :] = new
        return ref[...].astype(BF16)

    def scores(i):
        q8 = qn_ref[rows_of(i), :]
        qg = jnp.concatenate([q8[:, g * KV_DIM:(g + 1) * KV_DIM] for g in range(GROUP)], axis=0)
        qb = (jnp.concatenate([qg] * N_KV, axis=0) * kvmask).astype(BF16)
        knew = knew_ref[rows_of(i), :]
        sa = _dot_nt(qb, extended(kxa_ref, ck_ref, knew, 2 * i))
        sb = _dot_nt(qb, extended(kxb_ref, ck_ref, knew, 2 * i + 1))
        return sa * row_a + sb * row_b + bias

    nxt = scores(0)
    for i in range(SEQ_CHUNK // 2):
        s = nxt
        if i + 1 < SEQ_CHUNK // 2:
            nxt = scores(i + 1)
        pr = _softmax_sink(s, sink)
        vnew = vnew_ref[rows_of(i), :]
        o = (_dot((pr * row_a).astype(BF16), extended(vxa_ref, cv_ref, vnew, 2 * i))
             + _dot((pr * row_b).astype(BF16), extended(vxb_ref, cv_ref, vnew, 2 * i + 1))) * kvmask
        blk = GROUP * PAIR_ROWS
        og = o[0:blk] + o[blk:2 * blk] + o[2 * blk:3 * blk] + o[3 * blk:4 * blk]
        for g in range(GROUP):
            o_ref[rows_of(i), g * KV_DIM:(g + 1) * KV_DIM] = og[g * PAIR_ROWS:(g + 1) * PAIR_ROWS]


def _resident(shape, layer=None):
    zeros = (0,) * len(shape)
    if layer is None:
        return pl.BlockSpec(shape, lambda *_: zeros, pipeline_mode=pl.Buffered(1))
    return pl.BlockSpec((None,) + shape, lambda *_: (layer,) + zeros, pipeline_mode=pl.Buffered(1))


def _params(n_axes):
    return pltpu.CompilerParams(dimension_semantics=("arbitrary",) * n_axes,
                                vmem_limit_bytes=VMEM_LIMIT_BYTES)


def _ffn_weight_specs(i):
    return [_resident((1, D_MODEL), i), _resident((1, D_MODEL), i), _resident((D_MODEL, F2), i),
            _resident((CONV_W, F2), i), _resident((1, F2), i), _resident((D_FF, D_MODEL), i),
            _resident((D_MODEL, D_MODEL), i), _resident((PLE_DIM, D_MODEL), i)]


def _ffn_weights(W):
    return [W["norm_ffn"], W["norm_ple"], W["w_up"], W["conv_w"], W["conv_b"], W["w_down"],
            W["w_ple_gate"], W["w_ple_proj"]]


def _ffn_scratch_prompt(pre_c, rows):
    slabs = pltpu.VMEM((2 * FF_CHUNK // LANES, pre_c + rows, LANES), F32)
    return ([pltpu.VMEM((pre_c, F2), F32), pltpu.VMEM((D_MODEL // LANES, rows, LANES), F32)]
            + [slabs for _ in range(UP_AHEAD + 1)])


def _ffn_scratch(pre_c, rows):
    return ([pltpu.VMEM((pre_c, F2), F32)]
            + [pltpu.VMEM((pre_c + rows, 2 * FF_CHUNK), F32) for _ in range(UP_AHEAD + 1)])


def _prompt_specs(rows, n_tiles, total):
    mixer = lambda s: jnp.minimum(s, total - 1)
    ffn = lambda s: jnp.maximum(s - 1, 0)

    def make(which, per_sequence):
        def build(r, width, layer=None):
            lead = () if layer is None else (layer,)
            index = lambda s: lead + (which(s) // n_tiles, 0 if per_sequence else which(s) % n_tiles, 0)
            return pl.BlockSpec((None,) * (len(lead) + 1) + (r, width), index)
        return build

    return make(mixer, False), make(ffn, False), make(mixer, True), make(ffn, True)


def _kv_weight_specs():
    return [_resident((1, D_MODEL)), _resident((D_MODEL, KV_DIM)), _resident((D_MODEL, KV_DIM)),
            _resident((KV_DIM, KV_DIM)), _resident((1, KV_DIM))]


def _kv_weights(W, A):
    return [W["kv_norm"], A["w_k"], A["w_v"], A["bd_k"], A["knw"]]


def _pool_prompt_layer(h, p, W, i, *, rows):
    batch, length, _ = h.shape
    assert length % rows == 0
    tile = lambda width: pl.BlockSpec((None, rows, width), lambda b, t: (b, t, 0))
    state = lambda r, width: pl.BlockSpec((None, r, width), lambda b, t: (b, 0, 0))
    in_specs = [tile(D_MODEL), pl.BlockSpec((None, None, rows, PLE_DIM), lambda b, t: (i, b, t, 0)),
                _resident((1, D_MODEL), i),
                _resident((len(POOL_WINDOWS), POOL_GROUP, POOL_GROUP), i),
                _resident((1, D_MODEL), i)] + _ffn_weight_specs(i)
    args = [h, p, W["norm_mix"], W["w_pool"], W["pool_scale"]] + _ffn_weights(W)
    kern = functools.partial(_pool_prompt_kernel, rows=rows)
    return pl.pallas_call(
        kern,
        grid=(batch, length // rows),
        in_specs=in_specs,
        out_specs=[tile(D_MODEL), state(POOL_BUF, D_MODEL), state(CONV_BUF, F2)],
        out_shape=[jax.ShapeDtypeStruct(h.shape, F32),
                   jax.ShapeDtypeStruct((batch, POOL_BUF, D_MODEL), F32),
                   jax.ShapeDtypeStruct((batch, CONV_BUF, F2), F32)],
        scratch_shapes=[pltpu.VMEM((PROMPT_POOL_PRE, D_MODEL), F32)]
        + _ffn_scratch_prompt(PROMPT_CONV_PRE, rows),
        compiler_params=_params(2),
        name="pool_layer_prompt",
    )(*args)


def _attn_prompt_layer(h, p, k, v, W, A, i, *, rows):
    batch, length, _ = h.shape
    j = i - N_A
    n_tiles = length // rows
    total = batch * n_tiles
    blocks = rows // BLOCK
    assert n_tiles > 1 and length % rows == 0
    mixer_tile, ffn_tile, _, ffn_state = _prompt_specs(rows, n_tiles, total)
    mixer = lambda s: jnp.minimum(s, total - 1)
    prev = pl.BlockSpec(
        (None, BLOCK, KV_DIM),
        lambda s: (mixer(s) // n_tiles, jnp.maximum((mixer(s) % n_tiles) * blocks - 1, 0), 0))
    in_specs = [mixer_tile(rows, D_MODEL), ffn_tile(rows, PLE_DIM, i),
                prev, mixer_tile(rows, KV_DIM), prev, mixer_tile(rows, KV_DIM),
                _resident((1, D_MODEL), i), _resident((D_MODEL, D_MODEL), j),
                _resident((D_MODEL, LANES)), _resident((1, D_MODEL), j),
                _resident((2, N_HEADS, 2 * BLOCK, BLOCK)),
                _resident((N_KV, 1, GROUP * BLOCK), j), _resident((D_MODEL, D_MODEL), j)]
    in_specs += _ffn_weight_specs(i)
    args = [h, p, k, k, v, v, W["norm_mix"], A["w_q"], A["head_of_lane"], A["qnw"],
            A["bias_prompt"], A["sink_prompt"], A["w_o"]] + _ffn_weights(W)
    kern = functools.partial(_attn_prompt_kernel, rows=rows, n_tiles=n_tiles, total=total)
    khalf = pltpu.VMEM((N_KV // 2, rows + BLOCK, LANES), BF16)
    vhalf = pltpu.VMEM((N_KV // 2, 2 * LANES, rows + BLOCK), BF16)
    return pl.pallas_call(
        kern,
        grid=(total + 1,),
        in_specs=in_specs,
        out_specs=[ffn_tile(rows, D_MODEL), ffn_state(CONV_BUF, F2)],
        out_shape=[jax.ShapeDtypeStruct(h.shape, F32),
                   jax.ShapeDtypeStruct((batch, CONV_BUF, F2), F32)],
        scratch_shapes=[pltpu.VMEM((2, rows, D_MODEL), F32), khalf, khalf, vhalf, vhalf,
                        pltpu.VMEM((rows, D_MODEL), BF16)] + _ffn_scratch_prompt(PROMPT_CONV_PRE, rows),
        compiler_params=_params(1),
        name="attn_layer_prompt",
    )(*args)


def _decode_specs(rows, step):
    def build(r, width, layer=None):
        lead = () if layer is None else (layer,)
        return pl.BlockSpec((None,) * len(lead) + (r, step, width), lambda t: lead + (0, t, 0))
    return build


def _pool_decode_layer(h, p, ppre, cpre, W, i, *, rows, step):
    steps, n_seq, _ = h.shape
    block = _decode_specs(rows, step)
    in_specs = [block(steps, D_MODEL), block(steps, PLE_DIM, i), block(POOL_BUF, D_MODEL, i),
                block(CONV_BUF, F2, i), _resident((1, D_MODEL), i),
                _resident((len(POOL_WINDOWS), POOL_GROUP, POOL_GROUP), i),
                _resident((1, D_MODEL), i)] + _ffn_weight_specs(i)
    kern = functools.partial(_pool_decode_kernel, rows=rows, step=step)
    return pl.pallas_call(
        kern,
        grid=(n_seq // step,),
        in_specs=in_specs,
        out_specs=[block(steps, D_MODEL), block(POOL_BUF, D_MODEL), block(CONV_BUF, F2)],
        out_shape=[jax.ShapeDtypeStruct(h.shape, F32),
                   jax.ShapeDtypeStruct((POOL_BUF, n_seq, D_MODEL), F32),
                   jax.ShapeDtypeStruct((CONV_BUF, n_seq, F2), F32)],
        scratch_shapes=_ffn_scratch(CONV_BUF * step, rows),
        compiler_params=_params(1),
        name="pool_layer_decode",
    )(h, p, ppre, cpre, W["norm_mix"], W["w_pool"], W["pool_scale"], *_ffn_weights(W))


def _attn_post_decode_layer(h, o, p, cpre, W, A, i, *, rows, step):
    steps, n_seq, _ = h.shape
    j = i - N_A
    block = _decode_specs(rows, step)
    kern = functools.partial(_attn_post_decode_kernel, rows=rows, step=step)
    return pl.pallas_call(
        kern,
        grid=(n_seq // step,),
        in_specs=[block(steps, D_MODEL), block(steps, D_MODEL), block(steps, PLE_DIM, i),
                  block(CONV_BUF, F2, i), _resident((D_MODEL, D_MODEL), j)] + _ffn_weight_specs(i),
        out_specs=[block(steps, D_MODEL), block(CONV_BUF, F2)],
        out_shape=[jax.ShapeDtypeStruct(h.shape, F32),
                   jax.ShapeDtypeStruct((CONV_BUF, n_seq, F2), F32)],
        scratch_shapes=_ffn_scratch(CONV_BUF * step, rows),
        compiler_params=_params(1),
        name="attn_post_layer_decode",
    )(h, o, p, cpre, A["w_o"], *_ffn_weights(W))


def _shared_kv(h2d, W, A, *, rows):
    total = h2d.shape[0]
    tile = lambda width: pl.BlockSpec((rows, width), lambda t: (t, 0))
    return pl.pallas_call(
        _kv_kernel,
        grid=(total // rows,),
        in_specs=[tile(D_MODEL)] + _kv_weight_specs(),
        out_specs=[tile(KV_DIM), tile(KV_DIM)],
        out_shape=[jax.ShapeDtypeStruct((total, KV_DIM), F32)] * 2,
        compiler_params=_params(1),
        name="shared_kv",
    )(h2d, *_kv_weights(W, A))


def _decode_attn(h_bm, knew, vnew, cache_k, cache_v, W, A, i):
    rows = h_bm.shape[0]
    n_seq = cache_k.shape[0]
    j = i - N_A
    kpad = A["bias_decode"].shape[1]
    qrows = N_KV * GROUP * PAIR_ROWS
    full = lambda r, width: pl.BlockSpec((r, width), lambda c: (0, 0))
    cache = pl.BlockSpec((SEQ_CHUNK, WINDOW, KV_DIM), lambda c: (c, 0, 0))
    kern = functools.partial(_decode_attn_kernel, kpad=kpad)
    ext = pltpu.VMEM((kpad, KV_DIM), F32)
    return pl.pallas_call(
        kern,
        grid=(n_seq // SEQ_CHUNK,),
        in_specs=[full(rows, D_MODEL), full(rows, KV_DIM), full(rows, KV_DIM), cache, cache,
                  _resident((1, D_MODEL), i), _resident((D_MODEL, D_MODEL), j),
                  _resident((D_MODEL, D_MODEL)), _resident((1, D_MODEL), j),
                  _resident((qrows, kpad)), _resident((qrows, 1), j),
                  _resident((qrows, KV_DIM)), _resident((qrows, 1))],
        out_specs=full(rows, D_MODEL),
        out_shape=jax.ShapeDtypeStruct((rows, D_MODEL), F32),
        scratch_shapes=[pltpu.VMEM((rows, D_MODEL), F32), ext, ext, ext, ext],
        compiler_params=_params(1),
        name="decode_attn",
    )(h_bm, knew, vnew, cache_k, cache_v, W["norm_mix"], A["w_q"], A["bd_q"], A["qnw"],
      A["bias_decode"], A["sink_decode"], A["kvmask"], A["row_a"])


def _t5_bucket(d):
    n = jnp.maximum(d, 0)
    max_exact = N_BUCKETS // 2
    nf = jnp.maximum(n, 1).astype(F32)
    large = max_exact + (jnp.log(nf / max_exact) / math.log(MAX_DISTANCE / max_exact)
                         * (N_BUCKETS - max_exact)).astype(jnp.int32)
    large = jnp.minimum(large, N_BUCKETS - 1)
    return jnp.where(n < max_exact, n, large)


def _block_diag_ones(width):
    idx = np.arange(width) // HEAD_DIM
    return jnp.asarray(idx[:, None] == idx[None, :], BF16)


def _banded_bias(tab):
    heads = tab.shape[1]
    period = 3 * BLOCK + 1
    base = jnp.concatenate([jnp.full((heads, BLOCK - 1), NEG_INF, F32), tab.T,
                            jnp.full((heads, period - 2 * BLOCK + 1), NEG_INF, F32)], axis=1)
    flat = jnp.tile(base, (1, 2 * BLOCK))[:, :2 * BLOCK * (period - 1)]
    skew = flat.reshape(heads, 2 * BLOCK, period - 1)
    return skew[:, :, 2 * BLOCK - 1:3 * BLOCK - 1]


def _attention_tables(rel_bias, sinks):
    A = {}
    tab = rel_bias[_t5_bucket(jnp.arange(WINDOW))].astype(F32)

    steady = _banded_bias(tab)
    has_prev = (np.arange(2 * BLOCK) >= BLOCK)[None, :, None]
    A["bias_prompt"] = jnp.stack([jnp.where(has_prev, steady, NEG_INF), steady])

    tab3 = tab.reshape(WINDOW, N_KV, GROUP)
    steps = PAIR_ROWS // 2
    kpad = -(-(WINDOW + PAIR_ROWS) // BF16_SUBLANES) * BF16_SUBLANES
    col = np.arange(kpad)[None, :]
    r = np.arange(PAIR_ROWS)[:, None]
    step_of = r % steps
    seq_b = r >= steps
    keyidx = np.where(col < WINDOW, col, WINDOW + (col - WINDOW) % steps)
    own = (col < WINDOW) | ((col < WINDOW + PAIR_ROWS) & (((col - WINDOW) >= steps) == seq_b))
    dd = step_of + WINDOW - keyidx
    dvalid = own & (dd >= 0) & (dd < WINDOW)
    ddc = np.clip(dd, 0, WINDOW - 1)
    dec = jnp.where(dvalid[:, :, None, None], tab3[ddc], NEG_INF)
    A["bias_decode"] = dec.transpose(2, 3, 0, 1).reshape(N_KV * GROUP * PAIR_ROWS, kpad)
    A["row_a"] = jnp.asarray(np.tile(~seq_b, (N_KV * GROUP, 1)).reshape(-1, 1), F32)
    lane_kv = np.arange(KV_DIM)[None, :] // HEAD_DIM
    row_kv = np.arange(N_KV * GROUP * PAIR_ROWS)[:, None] // (GROUP * PAIR_ROWS)
    A["kvmask"] = jnp.asarray(lane_kv == row_kv, F32)

    sk = sinks.astype(F32).reshape(-1, N_KV, GROUP)
    A["sink_prompt"] = jnp.repeat(sk, BLOCK, axis=2)[:, :, None, :]
    A["sink_decode"] = jnp.repeat(sk.reshape(sk.shape[0], -1), PAIR_ROWS, axis=1)[..., None]
    return A


def _prepare(W):
    A = _attention_tables(W["rel_bias"], W["sinks"])
    n_b = W["w_q"].shape[0]
    A["w_q"] = (W["w_q"].reshape(n_b, D_MODEL, N_KV, GROUP, HEAD_DIM).transpose(0, 1, 3, 2, 4)
                .reshape(n_b, D_MODEL, D_MODEL).astype(BF16))
    A["w_o"] = (W["w_o"].reshape(n_b, N_KV, GROUP, HEAD_DIM, D_MODEL).transpose(0, 2, 1, 3, 4)
                .reshape(n_b, D_MODEL, D_MODEL).astype(BF16))
    A["qnw"] = jnp.tile(W["q_norm"], (1, N_HEADS)).reshape(n_b, 1, D_MODEL)
    A["knw"] = jnp.tile(W["k_norm"], N_KV).reshape(1, KV_DIM)
    A["w_k"] = W["w_k"].astype(BF16)
    A["w_v"] = W["w_v"].astype(BF16)
    A["bd_q"] = _block_diag_ones(D_MODEL)
    A["head_of_lane"] = jnp.asarray(
        (np.arange(D_MODEL) // HEAD_DIM)[:, None] == np.arange(LANES)[None, :], BF16)
    A["bd_k"] = _block_diag_ones(KV_DIM)
    V = dict(W)
    for name in ("w_pool", "w_up", "w_down", "w_ple_gate", "w_ple_proj"):
        V[name] = W[name].astype(BF16)
    for name in ("norm_mix", "norm_ffn", "norm_ple", "pool_scale", "conv_b"):
        V[name] = W[name][:, None, :]
    V["kv_norm"] = W["kv_norm"][None, :]
    return V, A


def kernel(x_prompt, x_sample, p_prompt, p_sample, state_pool, state_conv, cache_k, cache_v,
           norm_mix, norm_ffn, norm_ple, w_pool, pool_scale, kv_norm, w_k, w_v, k_norm,
           w_q, q_norm, sinks, w_o, rel_bias, w_up, conv_w, conv_b, w_down, w_ple_gate, w_ple_proj):
    W, A = _prepare(dict(
        norm_mix=norm_mix, norm_ffn=norm_ffn, norm_ple=norm_ple, w_pool=w_pool,
        pool_scale=pool_scale, kv_norm=kv_norm, w_k=w_k, w_v=w_v, k_norm=k_norm, w_q=w_q,
        q_norm=q_norm, sinks=sinks, w_o=w_o, rel_bias=rel_bias, w_up=w_up, conv_w=conv_w,
        conv_b=conv_b, w_down=w_down, w_ple_gate=w_ple_gate, w_ple_proj=w_ple_proj))
    batch, seq, _ = x_prompt.shape
    n_seq, steps, _ = x_sample.shape

    h = x_prompt
    pool_p, conv_p = [], []
    for i in range(N_A):
        h, ps, cs = _pool_prompt_layer(h, p_prompt, W, i, rows=PROMPT_TILE)
        pool_p.append(ps)
        conv_p.append(cs)
    k_p, v_p = _shared_kv(h.reshape(batch * seq, D_MODEL), W, A, rows=KV_TILE)
    k_p = k_p.reshape(batch, seq, KV_DIM)
    v_p = v_p.reshape(batch, seq, KV_DIM)
    for i in range(N_A, DEPTH):
        h, cs = _attn_prompt_layer(h, p_prompt, k_p, v_p, W, A, i, rows=PROMPT_TILE)
        conv_p.append(cs)
    y_prompt = h

    swap = lambda x: jnp.swapaxes(x, -3, -2)
    tile_rows = steps * DECODE_SEQ_TILE
    total = steps * n_seq
    hs = swap(x_sample)
    ps_sw, ppre_sw, cpre_sw = swap(p_sample), swap(state_pool), swap(state_conv)
    pool_s, conv_s = [], []
    for i in range(N_A):
        hs, ps, cs = _pool_decode_layer(hs, ps_sw, ppre_sw, cpre_sw, W, i, rows=tile_rows,
                                        step=DECODE_SEQ_TILE)
        pool_s.append(ps)
        conv_s.append(cs)
    hs_bm = swap(hs).reshape(total, D_MODEL)
    k_new, v_new = _shared_kv(hs_bm, W, A, rows=total)
    ck = cache_k.reshape(n_seq, WINDOW, KV_DIM)
    cv = cache_v.reshape(n_seq, WINDOW, KV_DIM)
    for i in range(N_A, DEPTH):
        o_bm = _decode_attn(hs_bm, k_new, v_new, ck, cv, W, A, i)
        hs, cs = _attn_post_decode_layer(hs, swap(o_bm.reshape(n_seq, steps, D_MODEL)), ps_sw,
                                         cpre_sw, W, A, i, rows=tile_rows, step=DECODE_SEQ_TILE)
        conv_s.append(cs)
        hs_bm = swap(hs).reshape(total, D_MODEL)
    y_sample = hs_bm.reshape(n_seq, steps, D_MODEL)

    def window(cache, new):
        ext = jnp.concatenate([cache, new.reshape(n_seq, steps, N_KV, HEAD_DIM)], axis=1)
        return ext[:, -WINDOW:]

    kp4 = k_p[:, -WINDOW:].reshape(batch, WINDOW, N_KV, HEAD_DIM)
    vp4 = v_p[:, -WINDOW:].reshape(batch, WINDOW, N_KV, HEAD_DIM)
    return (y_prompt, y_sample, jnp.stack(pool_p), swap(jnp.stack(pool_s)), jnp.stack(conv_p),
            swap(jnp.stack(conv_s)), kp4, window(cache_k, k_new), vp4, window(cache_v, v_new))
```
